```python
import jax, jax.numpy as jnp
from jax import lax
import numpy as np

D_MODEL = 1024
BATCH = 4
SEQ = 8192
DEPTH = 2

PLE_DIM = 256
MLA_HEADS = 8
MLA_NOPE = 64
MLA_ROPE = 32
MLA_V = 64
MLA_QK = MLA_NOPE + MLA_ROPE
MLA_Q_LORA = 384
MLA_KV_LORA = 128
ROPE_THETA = 10000.0
Q_BLOCK = 128
FNET_GROUPS = 4
GQA_Q_HEADS = 8
GQA_KV_HEADS = 2
GQA_GROUP = GQA_Q_HEADS // GQA_KV_HEADS
GQA_HEAD_DIM = 64
WINDOW = 128
WIN_BLOCK = 128
FFN_DIM = -(-8 * D_MODEL // (3 * 256)) * 256
RMS_EPS = 1e-6

IN_SPLITS = (MLA_Q_LORA, MLA_KV_LORA, MLA_ROPE,
             GQA_Q_HEADS * GQA_HEAD_DIM, GQA_KV_HEADS * GQA_HEAD_DIM, GQA_KV_HEADS * GQA_HEAD_DIM,
             D_MODEL, D_MODEL, D_MODEL)
IN_COLS = sum(IN_SPLITS)

kernel_name = 'hybrid_mla_fnet_swagqa_encoder'


def rms_norm(x, gain):
    xf = x.astype(jnp.float32)
    y = xf * lax.rsqrt(jnp.mean(xf * xf, axis=-1, keepdims=True) + RMS_EPS)
    return (y * gain.astype(jnp.float32)).astype(x.dtype)


def apply_rope(t, positions):
    half = t.shape[-1] // 2
    inv_freq = ROPE_THETA ** (-jnp.arange(half, dtype=jnp.float32) / half)
    ang = positions.astype(jnp.float32)[..., None] * inv_freq
    ang = ang.reshape(ang.shape[:2] + (1,) * (t.ndim - 3) + (half,))
    cos, sin = jnp.cos(ang), jnp.sin(ang)
    tf = t.astype(jnp.float32)
    t1, t2 = tf[..., :half], tf[..., half:]
    return jnp.concatenate([t1 * cos - t2 * sin, t1 * sin + t2 * cos], axis=-1).astype(t.dtype)


def alibi_slopes(n_heads):
    return 2.0 ** (-8.0 * (np.arange(n_heads, dtype=np.float32) + 1.0) / n_heads)


def mla_attention(c_q, c_kv, k_rope, positions, q_norm, w_uq, kv_norm, w_ukv):
    B, S, _ = c_q.shape
    q = (rms_norm(c_q, q_norm) @ w_uq).reshape(B, S, MLA_HEADS, MLA_QK)
    q_nope = q[..., :MLA_NOPE]
    q_rope = apply_rope(q[..., MLA_NOPE:], positions)
    kv = (rms_norm(c_kv, kv_norm) @ w_ukv).reshape(B, S, MLA_HEADS, MLA_NOPE + MLA_V)
    k_nope, v = kv[..., :MLA_NOPE], kv[..., MLA_NOPE:]
    k_r = apply_rope(k_rope, positions)
    scale = MLA_QK ** -0.5
    nb = S // Q_BLOCK
    qn_b = q_nope.reshape(B, nb, Q_BLOCK, MLA_HEADS, MLA_NOPE).transpose(1, 0, 2, 3, 4)
    qr_b = q_rope.reshape(B, nb, Q_BLOCK, MLA_HEADS, MLA_ROPE).transpose(1, 0, 2, 3, 4)

    def attend(blk):
        qn, qr = blk
        s = (jnp.einsum('bqhd,bkhd->bhqk', qn, k_nope)
             + jnp.einsum('bqhr,bkr->bhqk', qr, k_r)).astype(jnp.float32) * scale
        probs = jax.nn.softmax(s, axis=-1).astype(v.dtype)
        return jnp.einsum('bhqk,bkhd->bqhd', probs, v)

    o = lax.map(attend, (qn_b, qr_b))
    return o.transpose(1, 0, 2, 3, 4).reshape(B, S, MLA_HEADS * MLA_V)


def fourier_mix(h):
    B, S, D = h.shape
    hg = h.astype(jnp.float32).reshape(B, S, FNET_GROUPS, D // FNET_GROUPS)
    f = jnp.fft.fftn(hg, axes=(1, 3), norm='ortho').real
    return f.reshape(B, S, D).astype(h.dtype)


def windowed_gqa(q, k, v, positions, sink):
    B, S, _ = q.shape
    nb = S // WIN_BLOCK
    q = q.reshape(B, nb, WIN_BLOCK, GQA_KV_HEADS, GQA_GROUP, GQA_HEAD_DIM)
    k = k.reshape(B, S, GQA_KV_HEADS, GQA_HEAD_DIM)
    v = v.reshape(B, S, GQA_KV_HEADS, GQA_HEAD_DIM)

    def windows(t):
        pad = ((0, 0), (WIN_BLOCK, WIN_BLOCK)) + ((0, 0),) * (t.ndim - 2)
        tb = jnp.pad(t, pad).reshape((B, nb + 2, WIN_BLOCK) + t.shape[2:])
        return jnp.concatenate([tb[:, :-2], tb[:, 1:-1], tb[:, 2:]], axis=2)

    kw, vw, pw = windows(k), windows(v), windows(positions)
    scale = GQA_HEAD_DIM ** -0.5
    s = jnp.einsum('bnqhgd,bnkhd->bnhgqk', q, kw).astype(jnp.float32) * scale
    pq = positions.reshape(B, nb, WIN_BLOCK)
    dist = jnp.abs(pq[..., :, None] - pw[..., None, :]).astype(jnp.float32)
    slopes = jnp.asarray(alibi_slopes(GQA_Q_HEADS)).reshape(GQA_KV_HEADS, GQA_GROUP)
    s = s - slopes[None, None, :, :, None, None] * dist[:, :, None, None]
    qi = jnp.arange(WIN_BLOCK)
    kj = jnp.arange(3 * WIN_BLOCK)
    band = jnp.abs(kj[None, :] - WIN_BLOCK - qi[:, None]) <= WINDOW
    key_idx = jnp.arange(nb)[:, None] * WIN_BLOCK - WIN_BLOCK + kj[None, :]
    valid = (key_idx >= 0) & (key_idx < S)
    mask = band[None] & valid[:, None, :]
    s = jnp.where(mask[None, :, None, None], s, -jnp.inf)
    sink_l = sink.astype(jnp.float32).reshape(GQA_KV_HEADS, GQA_GROUP)[None, None, :, :, None, None]
    m = jnp.maximum(jnp.max(s, axis=-1, keepdims=True), sink_l)
    e = jnp.exp(s - m)
    probs = (e / (jnp.sum(e, axis=-1, keepdims=True) + jnp.exp(sink_l - m))).astype(v.dtype)
    o = jnp.einsum('bnhgqk,bnkhd->bnqhgd', probs, vw)
    return o.reshape(B, S, GQA_Q_HEADS * GQA_HEAD_DIM)


def setup_inputs(seed: int = 0) -> dict:
    key = jax.random.key(seed)
    ks = iter(jax.random.split(key, 32))

    def w(shape, fan_in):
        return jax.random.normal(next(ks), shape, jnp.float32) * (fan_in ** -0.5)

    def gain(n):
        return 1.0 + 0.1 * jax.random.normal(next(ks), (DEPTH, n), jnp.float32)

    x = jax.random.normal(next(ks), (BATCH, SEQ, D_MODEL), jnp.float32)
    p = jax.random.normal(next(ks), (DEPTH, BATCH, SEQ, PLE_DIM), jnp.float32)
    offset = jax.random.randint(next(ks), (BATCH, 1), 0, 4096, jnp.int32)
    positions = (offset + jnp.arange(SEQ, dtype=jnp.int32)[None, :]).astype(jnp.int32)
    return {
        'x': x,
        'p': p,
        'positions': positions,
        'norm_mix_pre': gain(D_MODEL),
        'w_in': w((DEPTH, D_MODEL, IN_COLS), D_MODEL),
        'mla_q_norm': gain(MLA_Q_LORA),
        'w_uq': w((DEPTH, MLA_Q_LORA, MLA_HEADS * MLA_QK), MLA_Q_LORA),
        'mla_kv_norm': gain(MLA_KV_LORA),
        'w_ukv': w((DEPTH, MLA_KV_LORA, MLA_HEADS * (MLA_NOPE + MLA_V)), MLA_KV_LORA),
        'gqa_sink': 0.5 * jax.random.normal(next(ks), (DEPTH, GQA_Q_HEADS), jnp.float32),
        'w_branch_a': w((DEPTH, MLA_HEADS * MLA_V, D_MODEL), MLA_HEADS * MLA_V),
        'w_branch_b': w((DEPTH, D_MODEL, D_MODEL), D_MODEL),
        'w_branch_c': w((DEPTH, GQA_Q_HEADS * GQA_HEAD_DIM, D_MODEL), GQA_Q_HEADS * GQA_HEAD_DIM),
        'w_out': w((DEPTH, D_MODEL, D_MODEL), D_MODEL),
        'norm_mix_post': gain(D_MODEL),
        'norm_ffn_pre': gain(D_MODEL),
        'w_ffn_gate': w((DEPTH, D_MODEL, FFN_DIM), D_MODEL),
        'w_ffn_up': w((DEPTH, D_MODEL, FFN_DIM), D_MODEL),
        'w_ffn_down': w((DEPTH, FFN_DIM, D_MODEL), FFN_DIM),
        'norm_ffn_post': gain(D_MODEL),
        'w_ple_proj': w((DEPTH, PLE_DIM, D_MODEL), PLE_DIM),
        'w_ple_gate': w((DEPTH, D_MODEL, D_MODEL), D_MODEL),
        'norm_ple': gain(D_MODEL),
    }


def reference(x, p, positions, norm_mix_pre, w_in, mla_q_norm, w_uq, mla_kv_norm, w_ukv,
              gqa_sink, w_branch_a, w_branch_b, w_branch_c, w_out, norm_mix_post,
              norm_ffn_pre, w_ffn_gate, w_ffn_up, w_ffn_down, norm_ffn_post,
              w_ple_proj, w_ple_gate, norm_ple):
    split_points = list(np.cumsum(IN_SPLITS)[:-1])
    for i in range(DEPTH):
        h = rms_norm(x, norm_mix_pre[i])
        z = h @ w_in[i]
        c_q, c_kv, k_rope, q_c, k_c, v_c, g_a, g_b, g_c = jnp.split(z, split_points, axis=-1)
        o_a = mla_attention(c_q, c_kv, k_rope, positions, mla_q_norm[i], w_uq[i],
                            mla_kv_norm[i], w_ukv[i])
        o_b = fourier_mix(h)
        o_c = windowed_gqa(q_c, k_c, v_c, positions, gqa_sink[i])
        merged = (jax.nn.sigmoid(g_a) * (o_a @ w_branch_a[i])
                  + jax.nn.sigmoid(g_b) * (o_b @ w_branch_b[i])
                  + jax.nn.sigmoid(g_c) * (o_c @ w_branch_c[i]))
        x = x + rms_norm(merged @ w_out[i], norm_mix_post[i])
        h = rms_norm(x, norm_ffn_pre[i])
        ff = (jax.nn.silu(h @ w_ffn_gate[i]) * (h @ w_ffn_up[i])) @ w_ffn_down[i]
        x = x + rms_norm(ff, norm_ffn_post[i])
        e = (p[i] @ w_ple_proj[i]) * jax.nn.sigmoid(x @ w_ple_gate[i])
        x = x + rms_norm(e, norm_ple[i])
    return x
```

```python
import functools
import math

import numpy as np
import jax
import jax.numpy as jnp
from jax import lax
from jax.experimental import pallas as pl
from jax.experimental.pallas import tpu as pltpu

F32 = jnp.float32
BF16 = jnp.bfloat16

D_MODEL = 1024
PLE_DIM = 256
MLA_HEADS = 8
MLA_NOPE = 64
MLA_ROPE = 32
MLA_V = 64
MLA_QK = MLA_NOPE + MLA_ROPE
MLA_Q_LORA = 384
MLA_KV_LORA = 128
ROPE_THETA = 10000.0
FNET_GROUPS = 4
FNET_GROUP_DIM = D_MODEL // FNET_GROUPS
GQA_Q_HEADS = 8
GQA_KV_HEADS = 2
GQA_GROUP = GQA_Q_HEADS // GQA_KV_HEADS
GQA_HEAD_DIM = 64
WINDOW = 128
WIN_BLOCK = 128
FFN_DIM = 2816
RMS_EPS = 1e-6

LANES = 128
HEAD_PAD = 128
FFT_N1 = 128
VMEM_LIMIT = 56 * 1024 * 1024

TOKEN_TILE = 512
FLASH_TQ = 512
FLASH_TK = 512
FFN_CHUNK = 256
LOG2E = 1.4426950408889634

SEG_CQ = (0, 384)
SEG_CKV = (384, 640)
SEG_QKV = (640, 1408)
SEG_GATE = (1408, 4480)
IN_COLS_PAD = 4480


def _params(n_parallel_axes):
    return pltpu.CompilerParams(
        dimension_semantics=("arbitrary",) * n_parallel_axes,
        vmem_limit_bytes=VMEM_LIMIT)


def _const_spec(shape):
    zeros = (0,) * len(shape)
    return pl.BlockSpec(shape, lambda *_: zeros, pipeline_mode=pl.Buffered(1))


def _rms(x, gain):
    ms = jnp.mean(x * x, axis=-1, keepdims=True)
    return x * lax.rsqrt(ms + RMS_EPS) * gain


def _dot(a, b):
    return jnp.dot(a, b, preferred_element_type=F32)


def _dot_nt(a, b):
    return lax.dot_general(a, b, (((1,), (1,)), ((), ())), preferred_element_type=F32)


def _inproj_kernel(x_ref, gain_ref, w_ref, dft_ref,
                   cq_ref, ckv_ref, qkv_ref, gate_ref, wr_ref, wi_ref):
    h = _rms(x_ref[...], gain_ref[...]).astype(BF16)
    cq_ref[...] = _dot(h, w_ref[:, SEG_CQ[0]:SEG_CQ[1]]).astype(BF16)
    ckv_ref[...] = _dot(h, w_ref[:, SEG_CKV[0]:SEG_CKV[1]]).astype(BF16)
    qkv_ref[...] = _dot(h, w_ref[:, SEG_QKV[0]:SEG_QKV[1]]).astype(BF16)
    for c in range(3):
        lo = SEG_GATE[0] + c * D_MODEL
        g = _dot(h, w_ref[:, lo:lo + D_MODEL])
        gate_ref[:, c * D_MODEL:(c + 1) * D_MODEL] = jax.nn.sigmoid(g).astype(BF16)
    gd = FNET_GROUP_DIM
    for g in range(FNET_GROUPS):
        yz = _dot(h[:, g * gd:(g + 1) * gd], dft_ref[...])
        wr_ref[:, g * gd:(g + 1) * gd] = yz[:, :gd].astype(BF16)
        wi_ref[:, g * gd:(g + 1) * gd] = yz[:, gd:].astype(BF16)


def _inproj(x2, gain, w_in_p, dft_c):
    t = x2.shape[0]
    tm = TOKEN_TILE
    row = lambda n: pl.BlockSpec((tm, n), lambda i: (i, 0))
    outs = [(t, 384), (t, 256), (t, 768), (t, 3 * D_MODEL), (t, D_MODEL), (t, D_MODEL)]
    return pl.pallas_call(
        _inproj_kernel,
        grid=(t // tm,),
        in_specs=[row(D_MODEL), _const_spec((1, D_MODEL)),
                  _const_spec((D_MODEL, IN_COLS_PAD)),
                  _const_spec((FNET_GROUP_DIM, 2 * FNET_GROUP_DIM))],
        out_specs=[row(n) for _, n in outs],
        out_shape=[jax.ShapeDtypeStruct(s, BF16) for s in outs],
        compiler_params=_params(1),
        name="inproj",
    )(x2, gain, w_in_p, dft_c)


def _mla_proj_kernel(cq_ref, ckv_ref, pos_ref, invf_ref, qn_ref, kvn_ref,
                     wq_ref, wqs_ref, wk_ref, wkr_ref, wkrs_ref, wv_ref, vb_ref,
                     q_ref, k_ref, v_ref, *, q_scale):
    cqn = _rms(cq_ref[...].astype(F32), qn_ref[...]).astype(BF16)
    ckvn = _rms(ckv_ref[:, :MLA_KV_LORA].astype(F32), kvn_ref[...]).astype(BF16)
    kr = ckv_ref[:, MLA_KV_LORA:]
    ang = pos_ref[...] * invf_ref[...]
    cos = jnp.cos(ang)
    sin = jnp.sin(ang)
    cos_q = cos * q_scale
    sin_q = sin * q_scale
    for h in range(MLA_HEADS):
        sl = slice(h * HEAD_PAD, (h + 1) * HEAD_PAD)
        qa = _dot(cqn, wq_ref[:, sl])
        qb = _dot(cqn, wqs_ref[:, sl])
        q_ref[:, sl] = (qa * cos_q + qb * sin_q).astype(BF16)
        ka = _dot(ckvn, wk_ref[:, sl]) + _dot(kr, wkr_ref[:, sl])
        kb = _dot(kr, wkrs_ref[:, sl])
        k_ref[:, sl] = (ka * cos + kb * sin).astype(BF16)
        v_ref[:, sl] = (_dot(ckvn, wv_ref[:, sl]) + vb_ref[:, sl]).astype(BF16)


def _mla_proj(cq, ckv, posf, invf_lane, qn, kvn, wq, wqs, wk, wkr, wkrs, wv, vb):
    t = cq.shape[0]
    tm = TOKEN_TILE
    hw = MLA_HEADS * HEAD_PAD
    row = lambda n: pl.BlockSpec((tm, n), lambda i: (i, 0))
    q_scale = (MLA_QK ** -0.5) * LOG2E
    return pl.pallas_call(
        functools.partial(_mla_proj_kernel, q_scale=q_scale),
        grid=(t // tm,),
        in_specs=[row(MLA_Q_LORA), row(256), row(1), _const_spec((1, LANES)),
                  _const_spec((1, MLA_Q_LORA)), _const_spec((1, MLA_KV_LORA)),
                  _const_spec((MLA_Q_LORA, hw)), _const_spec((MLA_Q_LORA, hw)),
                  _const_spec((MLA_KV_LORA, hw)), _const_spec((LANES, hw)),
                  _const_spec((LANES, hw)), _const_spec((MLA_KV_LORA, hw)),
                  _const_spec((1, hw))],
        out_specs=[row(hw)] * 3,
        out_shape=[jax.ShapeDtypeStruct((t, hw), BF16)] * 3,
        compiler_params=_params(1),
        name="mla_proj",
    )(cq, ckv, posf, invf_lane, qn, kvn, wq, wqs, wk, wkr, wkrs, wv, vb)


def _flash_kernel(q_ref, k_ref, v_ref, o_ref, *, tk, nk):
    outs = []
    for h in range(2):
        sl = slice(h * HEAD_PAD, (h + 1) * HEAD_PAD)
        q = q_ref[0, :, sl]
        tq = q.shape[0]

        def body(i, carry, sl=sl, q=q):
            m, acc = carry
            off = pl.multiple_of(i * tk, tk)
            k = k_ref[0, pl.ds(off, tk), sl]
            v = v_ref[0, pl.ds(off, tk), sl]
            s = _dot_nt(q, k)
            m_new = jnp.maximum(m, jnp.max(s, axis=-1, keepdims=True))
            p = jnp.exp2(s - m_new)
            alpha = jnp.exp2(m - m_new)
            acc = alpha * acc + _dot(p.astype(BF16), v)
            return m_new, acc

        m0 = jnp.full((tq, 1), -jnp.inf, F32)
        acc0 = jnp.zeros((tq, HEAD_PAD), F32)
        _, acc = lax.fori_loop(0, nk, body, (m0, acc0))
        outs.append(acc[:, :MLA_V] / acc[:, MLA_V:MLA_V + 1])
    o_ref[0] = jnp.concatenate(outs, axis=-1).astype(BF16)


def _flash(q, k, v):
    b, s, hw = q.shape
    tq, tk = FLASH_TQ, FLASH_TK
    pair = 2 * HEAD_PAD
    return pl.pallas_call(
        functools.partial(_flash_kernel, tk=tk, nk=s // tk),
        grid=(b, MLA_HEADS // 2, s // tq),
        in_specs=[pl.BlockSpec((1, tq, pair), lambda bi, hi, qi: (bi, qi, hi)),
                  pl.BlockSpec((1, s, pair), lambda bi, hi, qi: (bi, 0, hi)),
                  pl.BlockSpec((1, s, pair), lambda bi, hi, qi: (bi, 0, hi))],
        out_specs=pl.BlockSpec((1, tq, 2 * MLA_V), lambda bi, hi, qi: (bi, qi, hi)),
        out_shape=jax.ShapeDtypeStruct((b, s, MLA_HEADS * MLA_V), BF16),
        compiler_params=_params(3),
        name="mla_flash",
    )(q, k, v)


def _fft_a_kernel(xr_ref, xi_ref, m1_ref, ar_ref, ai_ref):
    x = jnp.concatenate([xr_ref[0], xi_ref[0]], axis=0)
    a = _dot(m1_ref[...], x)
    ar_ref[0] = a[:FFT_N1].astype(BF16)
    ai_ref[0] = a[FFT_N1:].astype(BF16)


def _fft_a(wr, wi, m1):
    b, n1, cols = wr.shape
    tn = 2048
    blk = pl.BlockSpec((1, n1, tn), lambda bi, ci: (bi, 0, ci))
    return pl.pallas_call(
        _fft_a_kernel,
        grid=(b, cols // tn),
        in_specs=[blk, blk, _const_spec((2 * n1, 2 * n1))],
        out_specs=[blk, blk],
        out_shape=[jax.ShapeDtypeStruct((b, n1, cols), BF16)] * 2,
        compiler_params=_params(2),
        name="fft_a",
    )(wr, wi, m1)


def _fft_b_kernel(ar_ref, ai_ref, d_ref, o_ref, *, kb, c):
    for j in range(kb):
        a = jnp.concatenate([ar_ref[0, j], ai_ref[0, j]], axis=0)
        o_ref[0, :, j * c:(j + 1) * c] = _dot(d_ref[j], a).astype(BF16)


def _fft_b(ar, ai, dmat):
    b, n1, n2, c = ar.shape
    kb = 8
    blk = pl.BlockSpec((1, kb, n2, c), lambda bi, ki: (bi, ki, 0, 0))
    return pl.pallas_call(
        functools.partial(_fft_b_kernel, kb=kb, c=c),
        grid=(b, n1 // kb),
        in_specs=[blk, blk, pl.BlockSpec((kb, n2, 2 * n2), lambda bi, ki: (ki, 0, 0))],
        out_specs=pl.BlockSpec((1, n2, kb * c), lambda bi, ki: (bi, 0, ki)),
        out_shape=jax.ShapeDtypeStruct((b, n2, n1 * c), BF16),
        compiler_params=_params(2),
        name="fft_b",
    )(ar, ai, dmat)


def _gqa_kernel(sink_ref, q_ref, kp_ref, kc_ref, kn_ref, vp_ref, vc_ref, vn_ref,
                pp_ref, pc_ref, pn_ref, o_ref, *, seq, slopes):
    j = pl.program_id(1)
    wb = WIN_BLOCK
    k3 = jnp.concatenate([kp_ref[0], kc_ref[0], kn_ref[0]], axis=0)
    v3 = jnp.concatenate([vp_ref[0], vc_ref[0], vn_ref[0]], axis=0)
    kj = lax.broadcasted_iota(jnp.int32, (wb, 3 * wb), 1)
    qi = lax.broadcasted_iota(jnp.int32, (wb, 3 * wb), 0)
    band = jnp.abs(kj - wb - qi) <= WINDOW
    key_idx = j * wb - wb + kj
    mask = band & (key_idx >= 0) & (key_idx < seq)
    pc = pc_ref[0].astype(F32)
    pq = jnp.broadcast_to(pc, (wb, wb)).T
    dist = jnp.concatenate(
        [jnp.abs(pq - pp_ref[0].astype(F32)), jnp.abs(pq - pc),
         jnp.abs(pq - pn_ref[0].astype(F32))], axis=1)
    scale = GQA_HEAD_DIM ** -0.5
    outs = []
    for hk in range(GQA_KV_HEADS):
        kh = k3[:, hk * GQA_HEAD_DIM:(hk + 1) * GQA_HEAD_DIM]
        vh = v3[:, hk * GQA_HEAD_DIM:(hk + 1) * GQA_HEAD_DIM]
        for g in range(GQA_GROUP):
            hq = hk * GQA_GROUP + g
            qh = q_ref[0, :, hq * GQA_HEAD_DIM:(hq + 1) * GQA_HEAD_DIM]
            s = _dot_nt(qh, kh) * scale - slopes[hq] * dist
            s = jnp.where(mask, s, -jnp.inf)
            sink = sink_ref[hq]
            m = jnp.maximum(jnp.max(s, axis=-1, keepdims=True), sink)
            e = jnp.exp(s - m)
            denom = jnp.sum(e, axis=-1, keepdims=True) + jnp.exp(sink - m)
            probs = (e / denom).astype(BF16)
            outs.append(_dot(probs, vh))
    o_ref[0] = jnp.concatenate(outs, axis=-1).astype(BF16)


def _gqa(qkv, pos_blocks, sink, slopes):
    b, s, _ = qkv.shape
    wb = WIN_BLOCK
    nb = s // wb
    prev = lambda j: jnp.maximum(j - 1, 0)
    nxt = lambda j: jnp.minimum(j + 1, nb - 1)
    kv = lambda lane_blk, f: pl.BlockSpec((1, wb, LANES), lambda bi, j: (bi, f(j), lane_blk))
    ps = lambda f: pl.BlockSpec((1, 1, wb), lambda bi, j: (bi * nb + f(j), 0, 0))
    same = lambda j: j
    qw = GQA_Q_HEADS * GQA_HEAD_DIM
    return pl.pallas_call(
        functools.partial(_gqa_kernel, seq=s, slopes=slopes),
        grid=(b, nb),
        in_specs=[pl.BlockSpec(memory_space=pltpu.SMEM),
                  pl.BlockSpec((1, wb, qw), lambda bi, j: (bi, j, 0)),
                  kv(4, prev), kv(4, same), kv(4, nxt),
                  kv(5, prev), kv(5, same), kv(5, nxt),
                  ps(prev), ps(same), ps(nxt)],
        out_specs=pl.BlockSpec((1, wb, qw), lambda bi, j: (bi, j, 0)),
        out_shape=jax.ShapeDtypeStruct((b, s, qw), BF16),
        compiler_params=_params(2),
        name="gqa_window",
    )(sink, qkv, qkv, qkv, qkv, qkv, qkv, qkv, pos_blocks, pos_blocks, pos_blocks)


def _token_kernel(x_ref, oa_ref, ob_ref, oc_ref, gate_ref, p_ref,
                  wa_ref, wb_ref, wc_ref, wo_ref, g_mix_ref,
                  g_pre_ref, wg_ref, wu_ref, wd_ref, g_post_ref,
                  wp_ref, wpg_ref, g_ple_ref, out_ref):
    d = D_MODEL
    merged = (gate_ref[:, 0:d].astype(F32) * _dot(oa_ref[...], wa_ref[...])
              + gate_ref[:, d:2 * d].astype(F32) * _dot(ob_ref[...], wb_ref[...])
              + gate_ref[:, 2 * d:3 * d].astype(F32) * _dot(oc_ref[...], wc_ref[...]))
    x = x_ref[...] + _rms(_dot(merged.astype(BF16), wo_ref[...]), g_mix_ref[...])

    h = _rms(x, g_pre_ref[...]).astype(BF16)
    ff = jnp.zeros(x.shape, F32)
    for c in range(FFN_DIM // FFN_CHUNK):
        sl = slice(c * FFN_CHUNK, (c + 1) * FFN_CHUNK)
        a = jax.nn.silu(_dot(h, wg_ref[:, sl])) * _dot(h, wu_ref[:, sl])
        ff = ff + _dot(a.astype(BF16), wd_ref[sl, :])
    x = x + _rms(ff, g_post_ref[...])

    e = _dot(p_ref[...].astype(BF16), wp_ref[...]) * jax.nn.sigmoid(
        _dot(x.astype(BF16), wpg_ref[...]))
    out_ref[...] = x + _rms(e, g_ple_ref[...])


def _token_tail(x2, oa, ob, oc, gates, p2, wa, wb, wc, wo, g_mix,
                g_pre, wg, wu, wd, g_post, wp, wpg, g_ple):
    t = x2.shape[0]
    tm = TOKEN_TILE
    d = D_MODEL
    row = lambda n: pl.BlockSpec((tm, n), lambda i: (i, 0))
    vec = _const_spec((1, d))
    return pl.pallas_call(
        _token_kernel,
        grid=(t // tm,),
        in_specs=[row(d), row(oa.shape[1]), row(d), row(oc.shape[1]), row(3 * d), row(PLE_DIM),
                  _const_spec(wa.shape), _const_spec(wb.shape), _const_spec(wc.shape),
                  _const_spec(wo.shape), vec,
                  vec, _const_spec(wg.shape), _const_spec(wu.shape), _const_spec(wd.shape), vec,
                  _const_spec(wp.shape), _const_spec(wpg.shape), vec],
        out_specs=row(d),
        out_shape=jax.ShapeDtypeStruct((t, d), F32),
        compiler_params=_params(1),
        name="token_tail",
    )(x2, oa, ob, oc, gates, p2, wa, wb, wc, wo, g_mix, g_pre, wg, wu, wd, g_post, wp, wpg, g_ple)


def _cos_sin(num, den):
    ang = (num % den).astype(F32) * (2.0 * math.pi / den)
    return jnp.cos(ang), jnp.sin(ang)


def _dft_tables(seq):
    gd = FNET_GROUP_DIM
    jk = jnp.arange(gd, dtype=jnp.int32)
    c, s = _cos_sin(jk[:, None] * jk[None, :], gd)
    chan = (jnp.concatenate([c, -s], axis=1) * gd ** -0.5).astype(BF16)

    n1, n2 = FFT_N1, seq // FFT_N1
    a = jnp.arange(n1, dtype=jnp.int32)
    c1, s1 = _cos_sin(a[:, None] * a[None, :], n1)
    m1 = (jnp.concatenate([jnp.concatenate([c1, s1], axis=1),
                           jnp.concatenate([-s1, c1], axis=1)], axis=0) * n1 ** -0.5).astype(BF16)

    k1 = jnp.arange(n1, dtype=jnp.int32)[:, None, None]
    k2 = jnp.arange(n2, dtype=jnp.int32)[None, :, None]
    nn = jnp.arange(n2, dtype=jnp.int32)[None, None, :]
    cd, sd = _cos_sin(nn * (k1 + n1 * k2), seq)
    dmat = (jnp.concatenate([cd, sd], axis=2) * n2 ** -0.5).astype(BF16)
    return chan, m1, dmat


def _head_blocks(parts):
    rows, heads = parts[0].shape[:2]
    used = sum(p.shape[2] for p in parts)
    pad = jnp.zeros((rows, heads, HEAD_PAD - used), parts[0].dtype)
    return jnp.concatenate(list(parts) + [pad], axis=2).reshape(rows, heads * HEAD_PAD)


def _prep_layer(i, w_in, w_uq, w_ukv):
    half = MLA_ROPE // 2
    wi = w_in[i]
    d = D_MODEL
    w_in_p = jnp.concatenate(
        [wi[:, :544], jnp.zeros((d, 96), wi.dtype), wi[:, 544:]], axis=1).astype(BF16)

    uq = w_uq[i].reshape(MLA_Q_LORA, MLA_HEADS, MLA_QK)
    nope, t1, t2 = uq[..., :MLA_NOPE], uq[..., MLA_NOPE:MLA_NOPE + half], uq[..., MLA_NOPE + half:]
    wq = _head_blocks([nope, t1, t2]).astype(BF16)
    wqs = _head_blocks([jnp.zeros_like(nope), -t2, t1]).astype(BF16)

    ukv = w_ukv[i].reshape(MLA_KV_LORA, MLA_HEADS, MLA_NOPE + MLA_V)
    wk = _head_blocks([ukv[..., :MLA_NOPE]]).astype(BF16)
    wv = _head_blocks([ukv[..., MLA_NOPE:]]).astype(BF16)

    eye = jnp.eye(LANES, MLA_ROPE, dtype=F32)
    e1, e2 = eye[:, :half], eye[:, half:]
    rep = lambda m: jnp.broadcast_to(m[:, None, :], (LANES, MLA_HEADS, m.shape[1]))
    z64 = jnp.zeros((LANES, MLA_HEADS, MLA_NOPE), F32)
    wkr = _head_blocks([z64, rep(e1), rep(e2)]).astype(BF16)
    wkrs = _head_blocks([z64, rep(-e2), rep(e1)]).astype(BF16)
    return w_in_p, wq, wqs, wk, wkr, wkrs, wv


def kernel(x, p, positions, norm_mix_pre, w_in, mla_q_norm, w_uq, mla_kv_norm, w_ukv, gqa_sink,
           w_branch_a, w_branch_b, w_branch_c, w_out, norm_mix_post, norm_ffn_pre, w_ffn_gate,
           w_ffn_up, w_ffn_down, norm_ffn_post, w_ple_proj, w_ple_gate, norm_ple):
    b, s, d = x.shape
    depth = w_in.shape[0]
    t = b * s
    n1, n2 = FFT_N1, s // FFT_N1
    assert d == D_MODEL and t % TOKEN_TILE == 0 and s % FLASH_TQ == 0 and s % FLASH_TK == 0
    assert (n2 * d) % 2048 == 0 and n2 % 16 == 0

    chan, m1, dmat = _dft_tables(s)
    half = MLA_ROPE // 2
    inv_freq = ROPE_THETA ** (-jnp.arange(half, dtype=F32) / half)
    invf_lane = jnp.zeros((1, LANES), F32)
    invf_lane = invf_lane.at[0, MLA_NOPE:MLA_NOPE + half].set(inv_freq)
    invf_lane = invf_lane.at[0, MLA_NOPE + half:MLA_NOPE + MLA_ROPE].set(inv_freq)
    posf = positions.reshape(t, 1).astype(F32)
    pos_blocks = positions.reshape(b * (s // WIN_BLOCK), 1, WIN_BLOCK)
    vbias = jnp.zeros((MLA_HEADS, HEAD_PAD), F32).at[:, MLA_V].set(1.0).reshape(1, -1)
    slopes = tuple(float(v) for v in
                   2.0 ** (-8.0 * (np.arange(GQA_Q_HEADS, dtype=np.float32) + 1.0) / GQA_Q_HEADS))
    r1 = lambda v: v.reshape(1, -1)
    bf = lambda w: w.astype(BF16)

    x2 = x.reshape(t, d)
    for i in range(depth):
        w_in_p, wq, wqs, wk, wkr, wkrs, wv = _prep_layer(i, w_in, w_uq, w_ukv)
        cq, ckv, qkv, gates, wr, wi = _inproj(x2, r1(norm_mix_pre[i]), w_in_p, chan)

        q, k, v = _mla_proj(cq, ckv, posf, invf_lane, r1(mla_q_norm[i]), r1(mla_kv_norm[i]),
                            wq, wqs, wk, wkr, wkrs, wv, vbias)
        hw = MLA_HEADS * HEAD_PAD
        o_a = _flash(q.reshape(b, s, hw), k.reshape(b, s, hw), v.reshape(b, s, hw))

        ar, ai = _fft_a(wr.reshape(b, n1, n2 * d), wi.reshape(b, n1, n2 * d), m1)
        o_b = _fft_b(ar.reshape(b, n1, n2, d), ai.reshape(b, n1, n2, d), dmat)

        o_c = _gqa(qkv.reshape(b, s, -1), pos_blocks, gqa_sink[i], slopes)

        x2 = _token_tail(
            x2, o_a.reshape(t, -1), o_b.reshape(t, d), o_c.reshape(t, -1), gates,
            p[i].reshape(t, PLE_DIM),
            bf(w_branch_a[i]), bf(w_branch_b[i]), bf(w_branch_c[i]), bf(w_out[i]),
            r1(norm_mix_post[i]), r1(norm_ffn_pre[i]), bf(w_ffn_gate[i]), bf(w_ffn_up[i]),
            bf(w_ffn_down[i]), r1(norm_ffn_post[i]), bf(w_ple_proj[i]), bf(w_ple_gate[i]),
            r1(norm_ple[i]))
    return x2.reshape(b, s, d)
```

```python
import functools
import math

import numpy as np
import jax
import jax.numpy as jnp
from jax import lax
from jax.experimental import pallas as pl
from jax.experimental.pallas import tpu as pltpu

F32 = jnp.float32
BF16 = jnp.bfloat16

D_MODEL = 1024
PLE_DIM = 256
MLA_HEADS = 8
MLA_NOPE = 64
MLA_ROPE = 32
MLA_V = 64
MLA_QK = MLA_NOPE + MLA_ROPE
MLA_Q_LORA = 384
MLA_KV_LORA = 128
ROPE_THETA = 10000.0
FNET_GROUPS = 4
FNET_GROUP_DIM = D_MODEL // FNET_GROUPS
GQA_Q_HEADS = 8
GQA_KV_HEADS = 2
GQA_GROUP = GQA_Q_HEADS // GQA_KV_HEADS
GQA_HEAD_DIM = 64
WINDOW = 128
WIN_BLOCK = 128
FFN_DIM = 2816
RMS_EPS = 1e-6

LANES = 128
HEAD_PAD = 128
FFT_N1 = 128
VMEM_LIMIT = 56 * 1024 * 1024

TOKEN_TILE = 512
FLASH_TQ = 512
FLASH_TK = 512
FLASH_UNROLL = 8
GQA_NQ = 8
FFN_CHUNK = 256
LOG2E = 1.4426950408889634

SEG_CQ = (0, 384)
SEG_CKV = (384, 640)
SEG_QKV = (640, 1664)
SEG_GATE = (1664, 4736)
IN_COLS_PAD = 4736


def _params(n_parallel_axes):
    return pltpu.CompilerParams(
        dimension_semantics=("arbitrary",) * n_parallel_axes,
        vmem_limit_bytes=VMEM_LIMIT)


def _const_spec(shape):
    zeros = (0,) * len(shape)
    return pl.BlockSpec(shape, lambda *_: zeros, pipeline_mode=pl.Buffered(1))


def _rms(x, gain):
    ms = jnp.mean(x * x, axis=-1, keepdims=True)
    return x * lax.rsqrt(ms + RMS_EPS) * gain


def _dot(a, b):
    return jnp.dot(a, b, preferred_element_type=F32)


def _dot_nt(a, b):
    return lax.dot_general(a, b, (((1,), (1,)), ((), ())), preferred_element_type=F32)


def _inproj_kernel(x_ref, gain_ref, w_ref, dft_ref,
                   cq_ref, ckv_ref, qkv_ref, gate_ref, wr_ref, wi_ref):
    h = _rms(x_ref[...], gain_ref[...]).astype(BF16)
    cq_ref[...] = _dot(h, w_ref[:, SEG_CQ[0]:SEG_CQ[1]]).astype(BF16)
    ckv_ref[...] = _dot(h, w_ref[:, SEG_CKV[0]:SEG_CKV[1]]).astype(BF16)
    qkv_ref[...] = _dot(h, w_ref[:, SEG_QKV[0]:SEG_QKV[1]]).astype(BF16)
    for c in range(3):
        lo = SEG_GATE[0] + c * D_MODEL
        g = _dot(h, w_ref[:, lo:lo + D_MODEL])
        gate_ref[:, c * D_MODEL:(c + 1) * D_MODEL] = jax.nn.sigmoid(g).astype(BF16)
    gd = FNET_GROUP_DIM
    for g in range(FNET_GROUPS):
        yz = _dot(h[:, g * gd:(g + 1) * gd], dft_ref[...])
        wr_ref[:, g * gd:(g + 1) * gd] = yz[:, :gd].astype(BF16)
        wi_ref[:, g * gd:(g + 1) * gd] = yz[:, gd:].astype(BF16)


def _inproj(x2, gain, w_in_p, dft_c):
    t = x2.shape[0]
    tm = TOKEN_TILE
    row = lambda n: pl.BlockSpec((tm, n), lambda i: (i, 0))
    outs = [(t, 384), (t, 256), (t, SEG_QKV[1] - SEG_QKV[0]), (t, 3 * D_MODEL), (t, D_MODEL),
            (t, D_MODEL)]
    return pl.pallas_call(
        _inproj_kernel,
        grid=(t // tm,),
        in_specs=[row(D_MODEL), _const_spec((1, D_MODEL)),
                  _const_spec((D_MODEL, IN_COLS_PAD)),
                  _const_spec((FNET_GROUP_DIM, 2 * FNET_GROUP_DIM))],
        out_specs=[row(n) for _, n in outs],
        out_shape=[jax.ShapeDtypeStruct(s, BF16) for s in outs],
        compiler_params=_params(1),
        name="inproj",
    )(x2, gain, w_in_p, dft_c)


def _mla_proj_kernel(cq_ref, ckv_ref, pos_ref, invf_ref, qn_ref, kvn_ref,
                     wq_ref, wqs_ref, wk_ref, wkr_ref, wkrs_ref, wv_ref, vb_ref,
                     q_ref, k_ref, v_ref, *, q_scale):
    cqn = _rms(cq_ref[...].astype(F32), qn_ref[...]).astype(BF16)
    ckvn = _rms(ckv_ref[:, :MLA_KV_LORA].astype(F32), kvn_ref[...]).astype(BF16)
    kr = ckv_ref[:, MLA_KV_LORA:]
    ang = pos_ref[...] * invf_ref[...]
    cos = jnp.cos(ang)
    sin = jnp.sin(ang)
    cos_q = cos * q_scale
    sin_q = sin * q_scale
    for h in range(MLA_HEADS):
        sl = slice(h * HEAD_PAD, (h + 1) * HEAD_PAD)
        qa = _dot(cqn, wq_ref[:, sl])
        qb = _dot(cqn, wqs_ref[:, sl])
        q_ref[:, sl] = (qa * cos_q + qb * sin_q).astype(BF16)
        ka = _dot(ckvn, wk_ref[:, sl]) + _dot(kr, wkr_ref[:, sl])
        kb = _dot(kr, wkrs_ref[:, sl])
        k_ref[:, sl] = (ka * cos + kb * sin).astype(BF16)
        v_ref[:, sl] = (_dot(ckvn, wv_ref[:, sl]) + vb_ref[:, sl]).astype(BF16)


def _mla_proj(cq, ckv, posf, invf_lane, qn, kvn, wq, wqs, wk, wkr, wkrs, wv, vb):
    t = cq.shape[0]
    tm = TOKEN_TILE
    hw = MLA_HEADS * HEAD_PAD
    row = lambda n: pl.BlockSpec((tm, n), lambda i: (i, 0))
    q_scale = (MLA_QK ** -0.5) * LOG2E
    return pl.pallas_call(
        functools.partial(_mla_proj_kernel, q_scale=q_scale),
        grid=(t // tm,),
        in_specs=[row(MLA_Q_LORA), row(256), row(1), _const_spec((1, LANES)),
                  _const_spec((1, MLA_Q_LORA)), _const_spec((1, MLA_KV_LORA)),
                  _const_spec((MLA_Q_LORA, hw)), _const_spec((MLA_Q_LORA, hw)),
                  _const_spec((MLA_KV_LORA, hw)), _const_spec((LANES, hw)),
                  _const_spec((LANES, hw)), _const_spec((MLA_KV_LORA, hw)),
                  _const_spec((1, hw))],
        out_specs=[row(hw)] * 3,
        out_shape=[jax.ShapeDtypeStruct((t, hw), BF16)] * 3,
        compiler_params=_params(1),
        name="mla_proj",
    )(cq, ckv, posf, invf_lane, qn, kvn, wq, wqs, wk, wkr, wkrs, wv, vb)


def _flash_kernel(q_ref, k_ref, v_ref, o_ref, *, tk, nk):
    tq = q_ref.shape[1]
    heads = [slice(h * HEAD_PAD, (h + 1) * HEAD_PAD) for h in range(2)]
    qs = [q_ref[0, :, sl] for sl in heads]

    def body(i, carry):
        off = pl.multiple_of(i * tk, tk)
        ss = [_dot_nt(q, k_ref[0, pl.ds(off, tk), sl]) for q, sl in zip(qs, heads)]
        new = []
        for (m, acc), s, sl in zip(carry, ss, heads):
            m_new = jnp.maximum(m, jnp.max(s, axis=-1, keepdims=True))
            p = jnp.exp2(s - m_new)
            alpha = jnp.exp2(m - m_new)
            v = v_ref[0, pl.ds(off, tk), sl]
            new.append((m_new, alpha * acc + _dot(p.astype(BF16), v)))
        return tuple(new)

    init = (jnp.full((tq, 1), -jnp.inf, F32), jnp.zeros((tq, HEAD_PAD), F32))
    final = lax.fori_loop(0, nk, body, (init, init), unroll=FLASH_UNROLL)
    outs = [acc[:, :MLA_V] / acc[:, MLA_V:MLA_V + 1] for _, acc in final]
    o_ref[0] = jnp.concatenate(outs, axis=-1).astype(BF16)


def _flash(q, k, v):
    b, s, hw = q.shape
    tq, tk = FLASH_TQ, FLASH_TK
    pair = 2 * HEAD_PAD
    return pl.pallas_call(
        functools.partial(_flash_kernel, tk=tk, nk=s // tk),
        grid=(b, MLA_HEADS // 2, s // tq),
        in_specs=[pl.BlockSpec((1, tq, pair), lambda bi, hi, qi: (bi, qi, hi)),
                  pl.BlockSpec((1, s, pair), lambda bi, hi, qi: (bi, 0, hi)),
                  pl.BlockSpec((1, s, pair), lambda bi, hi, qi: (bi, 0, hi))],
        out_specs=pl.BlockSpec((1, tq, 2 * MLA_V), lambda bi, hi, qi: (bi, qi, hi)),
        out_shape=jax.ShapeDtypeStruct((b, s, MLA_HEADS * MLA_V), BF16),
        compiler_params=_params(3),
        name="mla_flash",
    )(q, k, v)


def _fft_a_kernel(xr_ref, xi_ref, m1_ref, ar_ref, ai_ref):
    x = jnp.concatenate([xr_ref[0], xi_ref[0]], axis=0)
    a = _dot(m1_ref[...], x)
    ar_ref[0] = a[:FFT_N1].astype(BF16)
    ai_ref[0] = a[FFT_N1:].astype(BF16)


def _fft_a(wr, wi, m1):
    b, n1, cols = wr.shape
    tn = 2048
    blk = pl.BlockSpec((1, n1, tn), lambda bi, ci: (bi, 0, ci))
    return pl.pallas_call(
        _fft_a_kernel,
        grid=(b, cols // tn),
        in_specs=[blk, blk, _const_spec((2 * n1, 2 * n1))],
        out_specs=[blk, blk],
        out_shape=[jax.ShapeDtypeStruct((b, n1, cols), BF16)] * 2,
        compiler_params=_params(2),
        name="fft_a",
    )(wr, wi, m1)


def _fft_b_kernel(ar_ref, ai_ref, d_ref, o_ref, *, kb, c):
    for j in range(kb):
        a = jnp.concatenate([ar_ref[0, j], ai_ref[0, j]], axis=0)
        o_ref[0, :, j * c:(j + 1) * c] = _dot(d_ref[j], a).astype(BF16)


def _fft_b(ar, ai, dmat):
    b, n1, n2, c = ar.shape
    kb = 8
    blk = pl.BlockSpec((1, kb, n2, c), lambda bi, ki: (bi, ki, 0, 0))
    return pl.pallas_call(
        functools.partial(_fft_b_kernel, kb=kb, c=c),
        grid=(b, n1 // kb),
        in_specs=[blk, blk, pl.BlockSpec((kb, n2, 2 * n2), lambda bi, ki: (ki, 0, 0))],
        out_specs=pl.BlockSpec((1, n2, kb * c), lambda bi, ki: (bi, 0, ki)),
        out_shape=jax.ShapeDtypeStruct((b, n2, n1 * c), BF16),
        compiler_params=_params(2),
        name="fft_b",
    )(ar, ai, dmat)


def _gqa_kernel(sink_ref, q_ref, kp_ref, kc_ref, kn_ref, vp_ref, vc_ref, vn_ref,
                pp_ref, pc_ref, pn_ref, o_ref, k_scr, v_scr, p_scr, *, seq, nq, slopes):
    j = pl.program_id(1)
    wb = WIN_BLOCK
    r = nq * wb
    k_scr[0:wb] = kp_ref[0]
    k_scr[wb:wb + r] = kc_ref[0]
    k_scr[wb + r:] = kn_ref[0]
    v_scr[0:wb] = vp_ref[0]
    v_scr[wb:wb + r] = vc_ref[0]
    v_scr[wb + r:] = vn_ref[0]
    p_scr[0] = pp_ref[0].astype(F32)
    p_scr[1:nq + 1] = pc_ref[...].astype(F32)
    p_scr[nq + 1] = pn_ref[0].astype(F32)

    kj = lax.broadcasted_iota(jnp.int32, (wb, 3 * wb), 1)
    qi = lax.broadcasted_iota(jnp.int32, (wb, 3 * wb), 0)
    band = jnp.abs(kj - wb - qi) <= WINDOW
    low_half = lax.broadcasted_iota(jnp.int32, (3 * wb, LANES), 1) < GQA_HEAD_DIM

    def body(a, carry):
        row0 = pl.multiple_of(a * wb, wb)
        kw = k_scr[pl.ds(row0, 3 * wb), :]
        vw = v_scr[pl.ds(row0, 3 * wb), :]
        pc = p_scr[a + 1]
        pq = jnp.broadcast_to(pc, (wb, wb)).T
        dist = jnp.concatenate(
            [jnp.abs(pq - p_scr[a]), jnp.abs(pq - pc), jnp.abs(pq - p_scr[a + 2])], axis=1)
        key_idx = (j * nq + a) * wb - wb + kj
        mask = band & (key_idx >= 0) & (key_idx < seq)
        nd = jnp.where(mask, -dist, -jnp.inf)
        k_half, v_half = [], []
        for hk in range(GQA_KV_HEADS):
            kd = kw[:, hk * LANES:(hk + 1) * LANES]
            vd = vw[:, hk * LANES:(hk + 1) * LANES]
            zero = jnp.zeros_like(kd)
            k_half.append((jnp.where(low_half, kd, zero), jnp.where(low_half, zero, kd)))
            v_half.append((jnp.where(low_half, vd, zero), jnp.where(low_half, zero, vd)))
        scores = []
        for hq in range(GQA_Q_HEADS):
            pair, par, hk = hq // 2, hq % 2, hq // GQA_GROUP
            qp = q_ref[0, pl.ds(row0, wb), pair * LANES:(pair + 1) * LANES]
            scores.append(_dot_nt(qp, k_half[hk][par]) + (slopes[hq] * LOG2E) * nd)
        probs, rdenoms = [], []
        for hq, s in enumerate(scores):
            sink = sink_ref[hq] * LOG2E
            m = jnp.maximum(jnp.max(s, axis=-1, keepdims=True), sink)
            e = jnp.exp2(s - m)
            rdenoms.append(1.0 / (jnp.sum(e, axis=-1, keepdims=True) + jnp.exp2(sink - m)))
            probs.append(e.astype(BF16))
        outs = []
        for pair in range(GQA_Q_HEADS // 2):
            hk = (2 * pair) // GQA_GROUP
            even, odd = 2 * pair, 2 * pair + 1
            outs.append(_dot(probs[even], v_half[hk][0]) * rdenoms[even]
                        + _dot(probs[odd], v_half[hk][1]) * rdenoms[odd])
        o_ref[0, pl.ds(row0, wb), :] = jnp.concatenate(outs, axis=-1).astype(BF16)
        return carry

    lax.fori_loop(0, nq, body, 0)


def _gqa(qkv, pos_blocks, sink, slopes):
    b, s, _ = qkv.shape
    wb = WIN_BLOCK
    nb = s // wb
    nq = GQA_NQ
    r = nq * wb
    prev = lambda j: jnp.maximum(j * nq - 1, 0)
    nxt = lambda j: jnp.minimum((j + 1) * nq, nb - 1)
    edge = lambda lane_blk, f: pl.BlockSpec((1, wb, 2 * LANES), lambda bi, j: (bi, f(j), lane_blk))
    own = lambda lane_blk: pl.BlockSpec((1, r, 2 * LANES), lambda bi, j: (bi, j, lane_blk))
    pedge = lambda f: pl.BlockSpec((1, 1, wb), lambda bi, j: (bi * nb + f(j), 0, 0))
    qw = GQA_Q_HEADS * GQA_HEAD_DIM
    return pl.pallas_call(
        functools.partial(_gqa_kernel, seq=s, nq=nq, slopes=slopes),
        grid=(b, nb // nq),
        in_specs=[pl.BlockSpec(memory_space=pltpu.SMEM),
                  pl.BlockSpec((1, r, qw), lambda bi, j: (bi, j, 0)),
                  edge(2, prev), own(2), edge(2, nxt),
                  edge(3, prev), own(3), edge(3, nxt),
                  pedge(prev),
                  pl.BlockSpec((nq, 1, wb), lambda bi, j: (bi * (nb // nq) + j, 0, 0)),
                  pedge(nxt)],
        out_specs=pl.BlockSpec((1, r, qw), lambda bi, j: (bi, j, 0)),
        out_shape=jax.ShapeDtypeStruct((b, s, qw), BF16),
        scratch_shapes=[pltpu.VMEM((r + 2 * wb, 2 * LANES), BF16),
                        pltpu.VMEM((r + 2 * wb, 2 * LANES), BF16),
                        pltpu.VMEM((nq + 2, 1, wb), F32)],
        compiler_params=_params(2),
        name="gqa_window",
    )(sink, qkv, qkv, qkv, qkv, qkv, qkv, qkv, pos_blocks, pos_blocks, pos_blocks)


def _token_kernel(x_ref, oa_ref, ob_ref, oc_ref, gate_ref, p_ref,
                  wa_ref, wb_ref, wc_ref, wo_ref, g_mix_ref,
                  g_pre_ref, wg_ref, wu_ref, wd_ref, g_post_ref,
                  wp_ref, wpg_ref, g_ple_ref, out_ref):
    d = D_MODEL
    merged = (gate_ref[:, 0:d].astype(F32) * _dot(oa_ref[...], wa_ref[...])
              + gate_ref[:, d:2 * d].astype(F32) * _dot(ob_ref[...], wb_ref[...])
              + gate_ref[:, 2 * d:3 * d].astype(F32) * _dot(oc_ref[...], wc_ref[...]))
    x = x_ref[...] + _rms(_dot(merged.astype(BF16), wo_ref[...]), g_mix_ref[...])

    h = _rms(x, g_pre_ref[...]).astype(BF16)
    ff = jnp.zeros(x.shape, F32)
    for c in range(FFN_DIM // FFN_CHUNK):
        sl = slice(c * FFN_CHUNK, (c + 1) * FFN_CHUNK)
        a = jax.nn.silu(_dot(h, wg_ref[:, sl])) * _dot(h, wu_ref[:, sl])
        ff = ff + _dot(a.astype(BF16), wd_ref[sl, :])
    x = x + _rms(ff, g_post_ref[...])

    e = _dot(p_ref[...].astype(BF16), wp_ref[...]) * jax.nn.sigmoid(
        _dot(x.astype(BF16), wpg_ref[...]))
    out_ref[...] = x + _rms(e, g_ple_ref[...])


def _token_tail(x2, oa, ob, oc, gates, p2, wa, wb, wc, wo, g_mix,
                g_pre, wg, wu, wd, g_post, wp, wpg, g_ple):
    t = x2.shape[0]
    tm = TOKEN_TILE
    d = D_MODEL
    row = lambda n: pl.BlockSpec((tm, n), lambda i: (i, 0))
    vec = _const_spec((1, d))
    return pl.pallas_call(
        _token_kernel,
        grid=(t // tm,),
        in_specs=[row(d), row(oa.shape[1]), row(d), row(oc.shape[1]), row(3 * d), row(PLE_DIM),
                  _const_spec(wa.shape), _const_spec(wb.shape), _const_spec(wc.shape),
                  _const_spec(wo.shape), vec,
                  vec, _const_spec(wg.shape), _const_spec(wu.shape), _const_spec(wd.shape), vec,
                  _const_spec(wp.shape), _const_spec(wpg.shape), vec],
        out_specs=row(d),
        out_shape=jax.ShapeDtypeStruct((t, d), F32),
        compiler_params=_params(1),
        name="token_tail",
    )(x2, oa, ob, oc, gates, p2, wa, wb, wc, wo, g_mix, g_pre, wg, wu, wd, g_post, wp, wpg, g_ple)


def _cos_sin(num, den):
    ang = (num % den).astype(F32) * (2.0 * math.pi / den)
    return jnp.cos(ang), jnp.sin(ang)


def _dft_tables(seq):
    gd = FNET_GROUP_DIM
    jk = jnp.arange(gd, dtype=jnp.int32)
    c, s = _cos_sin(jk[:, None] * jk[None, :], gd)
    chan = (jnp.concatenate([c, -s], axis=1) * gd ** -0.5).astype(BF16)

    n1, n2 = FFT_N1, seq // FFT_N1
    a = jnp.arange(n1, dtype=jnp.int32)
    c1, s1 = _cos_sin(a[:, None] * a[None, :], n1)
    m1 = (jnp.concatenate([jnp.concatenate([c1, s1], axis=1),
                           jnp.concatenate([-s1, c1], axis=1)], axis=0) * n1 ** -0.5).astype(BF16)

    k1 = jnp.arange(n1, dtype=jnp.int32)[:, None, None]
    k2 = jnp.arange(n2, dtype=jnp.int32)[None, :, None]
    nn = jnp.arange(n2, dtype=jnp.int32)[None, None, :]
    cd, sd = _cos_sin(nn * (k1 + n1 * k2), seq)
    dmat = (jnp.concatenate([cd, sd], axis=2) * n2 ** -0.5).astype(BF16)
    return chan, m1, dmat


def _head_blocks(parts):
    rows, heads = parts[0].shape[:2]
    used = sum(p.shape[2] for p in parts)
    pad = jnp.zeros((rows, heads, HEAD_PAD - used), parts[0].dtype)
    return jnp.concatenate(list(parts) + [pad], axis=2).reshape(rows, heads * HEAD_PAD)


def _prep_layer(i, w_in, w_uq, w_ukv):
    half = MLA_ROPE // 2
    wi = w_in[i]
    d = D_MODEL
    qc = wi[:, 544:1056] * ((GQA_HEAD_DIM ** -0.5) * LOG2E)
    dup = lambda w: jnp.concatenate(
        [w[:, :GQA_HEAD_DIM], w[:, :GQA_HEAD_DIM], w[:, GQA_HEAD_DIM:], w[:, GQA_HEAD_DIM:]], axis=1)
    w_in_p = jnp.concatenate(
        [wi[:, :544], jnp.zeros((d, 96), wi.dtype), qc, dup(wi[:, 1056:1184]),
         dup(wi[:, 1184:1312]), wi[:, 1312:]], axis=1).astype(BF16)

    uq = w_uq[i].reshape(MLA_Q_LORA, MLA_HEADS, MLA_QK)
    nope, t1, t2 = uq[..., :MLA_NOPE], uq[..., MLA_NOPE:MLA_NOPE + half], uq[..., MLA_NOPE + half:]
    wq = _head_blocks([nope, t1, t2]).astype(BF16)
    wqs = _head_blocks([jnp.zeros_like(nope), -t2, t1]).astype(BF16)

    ukv = w_ukv[i].reshape(MLA_KV_LORA, MLA_HEADS, MLA_NOPE + MLA_V)
    wk = _head_blocks([ukv[..., :MLA_NOPE]]).astype(BF16)
    wv = _head_blocks([ukv[..., MLA_NOPE:]]).astype(BF16)

    eye = jnp.eye(LANES, MLA_ROPE, dtype=F32)
    e1, e2 = eye[:, :half], eye[:, half:]
    rep = lambda m: jnp.broadcast_to(m[:, None, :], (LANES, MLA_HEADS, m.shape[1]))
    z64 = jnp.zeros((LANES, MLA_HEADS, MLA_NOPE), F32)
    wkr = _head_blocks([z64, rep(e1), rep(e2)]).astype(BF16)
    wkrs = _head_blocks([z64, rep(-e2), rep(e1)]).astype(BF16)
    return w_in_p, wq, wqs, wk, wkr, wkrs, wv


def kernel(x, p, positions, norm_mix_pre, w_in, mla_q_norm, w_uq, mla_kv_norm, w_ukv, gqa_sink,
           w_branch_a, w_branch_b, w_branch_c, w_out, norm_mix_post, norm_ffn_pre, w_ffn_gate,
           w_ffn_up, w_ffn_down, norm_ffn_post, w_ple_proj, w_ple_gate, norm_ple):
    b, s, d = x.shape
    depth = w_in.shape[0]
    t = b * s
    n1, n2 = FFT_N1, s // FFT_N1
    assert d == D_MODEL and t % TOKEN_TILE == 0 and s % FLASH_TQ == 0 and s % FLASH_TK == 0
    assert (n2 * d) % 2048 == 0 and n2 % 16 == 0 and (s // WIN_BLOCK) % GQA_NQ == 0

    chan, m1, dmat = _dft_tables(s)
    half = MLA_ROPE // 2
    inv_freq = ROPE_THETA ** (-jnp.arange(half, dtype=F32) / half)
    invf_lane = jnp.zeros((1, LANES), F32)
    invf_lane = invf_lane.at[0, MLA_NOPE:MLA_NOPE + half].set(inv_freq)
    invf_lane = invf_lane.at[0, MLA_NOPE + half:MLA_NOPE + MLA_ROPE].set(inv_freq)
    posf = positions.reshape(t, 1).astype(F32)
    pos_blocks = positions.reshape(b * (s // WIN_BLOCK), 1, WIN_BLOCK)
    vbias = jnp.zeros((MLA_HEADS, HEAD_PAD), F32).at[:, MLA_V].set(1.0).reshape(1, -1)
    slopes = tuple(float(v) for v in
                   2.0 ** (-8.0 * (np.arange(GQA_Q_HEADS, dtype=np.float32) + 1.0) / GQA_Q_HEADS))
    r1 = lambda v: v.reshape(1, -1)
    bf = lambda w: w.astype(BF16)

    x2 = x.reshape(t, d)
    for i in range(depth):
        w_in_p, wq, wqs, wk, wkr, wkrs, wv = _prep_layer(i, w_in, w_uq, w_ukv)
        cq, ckv, qkv, gates, wr, wi = _inproj(x2, r1(norm_mix_pre[i]), w_in_p, chan)

        q, k, v = _mla_proj(cq, ckv, posf, invf_lane, r1(mla_q_norm[i]), r1(mla_kv_norm[i]),
                            wq, wqs, wk, wkr, wkrs, wv, vbias)
        hw = MLA_HEADS * HEAD_PAD
        o_a = _flash(q.reshape(b, s, hw), k.reshape(b, s, hw), v.reshape(b, s, hw))

        ar, ai = _fft_a(wr.reshape(b, n1, n2 * d), wi.reshape(b, n1, n2 * d), m1)
        o_b = _fft_b(ar.reshape(b, n1, n2, d), ai.reshape(b, n1, n2, d), dmat)

        o_c = _gqa(qkv.reshape(b, s, -1), pos_blocks, gqa_sink[i], slopes)

        x2 = _token_tail(
            x2, o_a.reshape(t, -1), o_b.reshape(t, d), o_c.reshape(t, -1), gates,
            p[i].reshape(t, PLE_DIM),
            bf(w_branch_a[i]), bf(w_branch_b[i]), bf(w_branch_c[i]), bf(w_out[i]),
            r1(norm_mix_post[i]), r1(norm_ffn_pre[i]), bf(w_ffn_gate[i]), bf(w_ffn_up[i]),
            bf(w_ffn_down[i]), r1(norm_ffn_post[i]), bf(w_ple_proj[i]), bf(w_ple_gate[i]),
            r1(norm_ple[i]))
    return x2.reshape(b, s, d)
```

```python
import functools
import math

import numpy as np
import jax
import jax.numpy as jnp
from jax import lax
from jax.experimental import pallas as pl
from jax.experimental.pallas import tpu as pltpu

F32 = jnp.float32
BF16 = jnp.bfloat16

D_MODEL = 1024
PLE_DIM = 256
MLA_HEADS = 8
MLA_NOPE = 64
MLA_ROPE = 32
MLA_V = 64
MLA_QK = MLA_NOPE + MLA_ROPE
MLA_Q_LORA = 384
MLA_KV_LORA = 128
ROPE_THETA = 10000.0
FNET_GROUPS = 4
FNET_GROUP_DIM = D_MODEL // FNET_GROUPS
GQA_Q_HEADS = 8
GQA_KV_HEADS = 2
GQA_GROUP = GQA_Q_HEADS // GQA_KV_HEADS
GQA_HEAD_DIM = 64
WINDOW = 128
WIN_BLOCK = 128
FFN_DIM = 2816
RMS_EPS = 1e-6

LANES = 128
HEAD_PAD = 128
FFT_N1 = 128
VMEM_LIMIT = 56 * 1024 * 1024

TOKEN_TILE = 512
FLASH_TQ = 512
FLASH_TK = 1024
GQA_NQ = 8
FFT_A_COLS = 8192
FFT_B_ROWS = 16
FFN_CHUNK = 256
LOG2E = 1.4426950408889634

SEG_CQ = (0, 384)
SEG_CKV = (384, 640)
SEG_QKV = (640, 1664)
SEG_GATE = (1664, 4736)
IN_COLS_PAD = 4736


def _params(n_parallel_axes):
    return pltpu.CompilerParams(
        dimension_semantics=("arbitrary",) * n_parallel_axes,
        vmem_limit_bytes=VMEM_LIMIT)


def _const_spec(shape):
    zeros = (0,) * len(shape)
    return pl.BlockSpec(shape, lambda *_: zeros, pipeline_mode=pl.Buffered(1))


def _rms(x, gain):
    ms = jnp.mean(x * x, axis=-1, keepdims=True)
    return x * lax.rsqrt(ms + RMS_EPS) * gain


def _dot(a, b):
    return jnp.dot(a, b, preferred_element_type=F32)


def _dot_nt(a, b):
    return lax.dot_general(a, b, (((1,), (1,)), ((), ())), preferred_element_type=F32)


def _inproj_kernel(x_ref, gain_ref, w_ref, dft_ref,
                   cq_ref, ckv_ref, qkv_ref, gate_ref, wr_ref, wi_ref):
    h = _rms(x_ref[...], gain_ref[...]).astype(BF16)
    cq_ref[...] = _dot(h, w_ref[:, SEG_CQ[0]:SEG_CQ[1]]).astype(BF16)
    ckv_ref[...] = _dot(h, w_ref[:, SEG_CKV[0]:SEG_CKV[1]]).astype(BF16)
    qkv_ref[...] = _dot(h, w_ref[:, SEG_QKV[0]:SEG_QKV[1]]).astype(BF16)
    for c in range(3):
        lo = SEG_GATE[0] + c * D_MODEL
        g = _dot(h, w_ref[:, lo:lo + D_MODEL])
        gate_ref[:, c * D_MODEL:(c + 1) * D_MODEL] = jax.nn.sigmoid(g).astype(BF16)
    gd = FNET_GROUP_DIM
    for g in range(FNET_GROUPS):
        yz = _dot(h[:, g * gd:(g + 1) * gd], dft_ref[...])
        wr_ref[:, g * gd:(g + 1) * gd] = yz[:, :gd].astype(BF16)
        wi_ref[:, g * gd:(g + 1) * gd] = yz[:, gd:].astype(BF16)


def _inproj(x2, gain, w_in_p, dft_c):
    t = x2.shape[0]
    tm = TOKEN_TILE
    row = lambda n: pl.BlockSpec((tm, n), lambda i: (i, 0))
    outs = [(t, 384), (t, 256), (t, SEG_QKV[1] - SEG_QKV[0]), (t, 3 * D_MODEL), (t, D_MODEL),
            (t, D_MODEL)]
    return pl.pallas_call(
        _inproj_kernel,
        grid=(t // tm,),
        in_specs=[row(D_MODEL), _const_spec((1, D_MODEL)),
                  _const_spec((D_MODEL, IN_COLS_PAD)),
                  _const_spec((FNET_GROUP_DIM, 2 * FNET_GROUP_DIM))],
        out_specs=[row(n) for _, n in outs],
        out_shape=[jax.ShapeDtypeStruct(s, BF16) for s in outs],
        compiler_params=_params(1),
        name="inproj",
    )(x2, gain, w_in_p, dft_c)


def _rope_table_kernel(pos_ref, invf_ref, cos_ref, sin_ref):
    ang = pos_ref[...] * invf_ref[...]
    cos_ref[...] = jnp.cos(ang)
    sin_ref[...] = jnp.sin(ang)


def _rope_table(posf, invf_lane):
    t = posf.shape[0]
    tm = TOKEN_TILE
    row = lambda n: pl.BlockSpec((tm, n), lambda i: (i, 0))
    return pl.pallas_call(
        _rope_table_kernel,
        grid=(t // tm,),
        in_specs=[row(1), _const_spec((1, LANES))],
        out_specs=[row(LANES)] * 2,
        out_shape=[jax.ShapeDtypeStruct((t, LANES), F32)] * 2,
        compiler_params=_params(1),
        name="rope_table",
    )(posf, invf_lane)


def _mla_proj_kernel(cq_ref, ckv_ref, cos_ref, sin_ref, qn_ref, kvn_ref,
                     wq_ref, wk_ref, wv_ref, vb_ref, q_ref, k_ref, v_ref):
    cqn = _rms(cq_ref[...].astype(F32), qn_ref[...]).astype(BF16)
    ckvn = _rms(ckv_ref[:, :MLA_KV_LORA].astype(F32), kvn_ref[...]).astype(BF16)
    cos = cos_ref[...]
    sin = sin_ref[...]
    half = MLA_ROPE // 2
    lane = lax.broadcasted_iota(jnp.int32, cos.shape, 1)
    second = lane >= MLA_NOPE + half
    sin_up = jnp.where(second, sin, 0.0)
    sin_dn = jnp.where(second, 0.0, -sin)

    def rope(x):
        return (x * cos + pltpu.roll(x, half, 1) * sin_up
                + pltpu.roll(x, HEAD_PAD - half, 1) * sin_dn)

    kr = rope(ckv_ref[:, MLA_KV_LORA:].astype(F32))
    for h in range(MLA_HEADS):
        sl = slice(h * HEAD_PAD, (h + 1) * HEAD_PAD)
        q_ref[:, sl] = rope(_dot(cqn, wq_ref[:, sl])).astype(BF16)
        k_ref[:, sl] = (_dot(ckvn, wk_ref[:, sl]) + kr).astype(BF16)
        v_ref[:, sl] = (_dot(ckvn, wv_ref[:, sl]) + vb_ref[:, sl]).astype(BF16)


def _mla_proj(cq, ckv, cos, sin, qn, kvn, wq, wk, wv, vb):
    t = cq.shape[0]
    tm = TOKEN_TILE
    hw = MLA_HEADS * HEAD_PAD
    row = lambda n: pl.BlockSpec((tm, n), lambda i: (i, 0))
    return pl.pallas_call(
        _mla_proj_kernel,
        grid=(t // tm,),
        in_specs=[row(MLA_Q_LORA), row(256), row(LANES), row(LANES),
                  _const_spec((1, MLA_Q_LORA)), _const_spec((1, MLA_KV_LORA)),
                  _const_spec((MLA_Q_LORA, hw)), _const_spec((MLA_KV_LORA, hw)),
                  _const_spec((MLA_KV_LORA, hw)), _const_spec((1, hw))],
        out_specs=[row(hw)] * 3,
        out_shape=[jax.ShapeDtypeStruct((t, hw), BF16)] * 3,
        compiler_params=_params(1),
        name="mla_proj",
    )(cq, ckv, cos, sin, qn, kvn, wq, wk, wv, vb)


def _flash_kernel(q_ref, k_ref, v_ref, o_ref, *, tk, nk):
    tq = q_ref.shape[1]
    heads = [slice(h * HEAD_PAD, (h + 1) * HEAD_PAD) for h in range(2)]
    qs = [q_ref[0, :, sl] for sl in heads]

    def scores(i):
        return [_dot_nt(q, k_ref[0, i * tk:(i + 1) * tk, sl]) for q, sl in zip(qs, heads)]

    state = [(jnp.full((tq, 1), -jnp.inf, F32), jnp.zeros((tq, HEAD_PAD), F32))] * 2
    ss = scores(0)
    for i in range(nk):
        ss_next = scores(i + 1) if i + 1 < nk else None
        new = []
        for (m, acc), s, sl in zip(state, ss, heads):
            m_new = jnp.maximum(m, jnp.max(s, axis=-1, keepdims=True))
            p = jnp.exp2((s - m_new).astype(BF16))
            alpha = jnp.exp2(m - m_new)
            v = v_ref[0, i * tk:(i + 1) * tk, sl]
            new.append((m_new, alpha * acc + _dot(p, v)))
        state, ss = new, ss_next
    outs = [acc[:, :MLA_V] / acc[:, MLA_V:MLA_V + 1] for _, acc in state]
    o_ref[0] = jnp.concatenate(outs, axis=-1).astype(BF16)


def _flash(q, k, v):
    b, s, hw = q.shape
    tq, tk = FLASH_TQ, FLASH_TK
    pair = 2 * HEAD_PAD
    return pl.pallas_call(
        functools.partial(_flash_kernel, tk=tk, nk=s // tk),
        grid=(b, MLA_HEADS // 2, s // tq),
        in_specs=[pl.BlockSpec((1, tq, pair), lambda bi, hi, qi: (bi, qi, hi)),
                  pl.BlockSpec((1, s, pair), lambda bi, hi, qi: (bi, 0, hi)),
                  pl.BlockSpec((1, s, pair), lambda bi, hi, qi: (bi, 0, hi))],
        out_specs=pl.BlockSpec((1, tq, 2 * MLA_V), lambda bi, hi, qi: (bi, qi, hi)),
        out_shape=jax.ShapeDtypeStruct((b, s, MLA_HEADS * MLA_V), BF16),
        compiler_params=_params(3),
        name="mla_flash",
    )(q, k, v)


def _fft_a_kernel(xr_ref, xi_ref, m1_ref, ar_ref, ai_ref):
    x = jnp.concatenate([xr_ref[0], xi_ref[0]], axis=0)
    a = _dot(m1_ref[...], x)
    ar_ref[0] = a[:FFT_N1].astype(BF16)
    ai_ref[0] = a[FFT_N1:].astype(BF16)


def _fft_a(wr, wi, m1):
    b, n1, cols = wr.shape
    tn = FFT_A_COLS
    blk = pl.BlockSpec((1, n1, tn), lambda bi, ci: (bi, 0, ci))
    return pl.pallas_call(
        _fft_a_kernel,
        grid=(b, cols // tn),
        in_specs=[blk, blk, _const_spec((2 * n1, 2 * n1))],
        out_specs=[blk, blk],
        out_shape=[jax.ShapeDtypeStruct((b, n1, cols), BF16)] * 2,
        compiler_params=_params(2),
        name="fft_a",
    )(wr, wi, m1)


def _fft_b_kernel(ar_ref, ai_ref, d_ref, o_ref, *, kb, c):
    for j in range(kb):
        a = jnp.concatenate([ar_ref[0, j], ai_ref[0, j]], axis=0)
        o_ref[0, :, j * c:(j + 1) * c] = _dot(d_ref[j], a).astype(BF16)


def _fft_b(ar, ai, dmat):
    b, n1, n2, c = ar.shape
    kb = FFT_B_ROWS
    blk = pl.BlockSpec((1, kb, n2, c), lambda bi, ki: (bi, ki, 0, 0))
    return pl.pallas_call(
        functools.partial(_fft_b_kernel, kb=kb, c=c),
        grid=(b, n1 // kb),
        in_specs=[blk, blk, pl.BlockSpec((kb, n2, 2 * n2), lambda bi, ki: (ki, 0, 0))],
        out_specs=pl.BlockSpec((1, n2, kb * c), lambda bi, ki: (bi, 0, ki)),
        out_shape=jax.ShapeDtypeStruct((b, n2, n1 * c), BF16),
        compiler_params=_params(2),
        name="fft_b",
    )(ar, ai, dmat)


def _gqa_kernel(sink_ref, q_ref, kp_ref, kc_ref, kn_ref, vp_ref, vc_ref, vn_ref,
                pp_ref, pc_ref, pn_ref, o_ref, k_scr, v_scr, p_scr, *, seq, nq, slopes):
    j = pl.program_id(1)
    wb = WIN_BLOCK
    r = nq * wb
    k_scr[0:wb] = kp_ref[0]
    k_scr[wb:wb + r] = kc_ref[0]
    k_scr[wb + r:] = kn_ref[0]
    v_scr[0:wb] = vp_ref[0]
    v_scr[wb:wb + r] = vc_ref[0]
    v_scr[wb + r:] = vn_ref[0]
    p_scr[0] = pp_ref[0].astype(F32)
    p_scr[1:nq + 1] = pc_ref[...].astype(F32)
    p_scr[nq + 1] = pn_ref[0].astype(F32)

    kj = lax.broadcasted_iota(jnp.int32, (wb, 3 * wb), 1)
    qi = lax.broadcasted_iota(jnp.int32, (wb, 3 * wb), 0)
    band = jnp.abs(kj - wb - qi) <= WINDOW
    low_half = lax.broadcasted_iota(jnp.int32, (3 * wb, LANES), 1) < GQA_HEAD_DIM

    def body(a, carry):
        row0 = pl.multiple_of(a * wb, wb)
        kw = k_scr[pl.ds(row0, 3 * wb), :]
        vw = v_scr[pl.ds(row0, 3 * wb), :]
        pc = p_scr[a + 1]
        pq = jnp.broadcast_to(pc, (wb, wb)).T
        dist = jnp.concatenate(
            [jnp.abs(pq - p_scr[a]), jnp.abs(pq - pc), jnp.abs(pq - p_scr[a + 2])], axis=1)
        key_idx = (j * nq + a) * wb - wb + kj
        mask = band & (key_idx >= 0) & (key_idx < seq)
        nd = jnp.where(mask, -dist, -jnp.inf)
        k_half, v_half = [], []
        for hk in range(GQA_KV_HEADS):
            kd = kw[:, hk * LANES:(hk + 1) * LANES]
            vd = vw[:, hk * LANES:(hk + 1) * LANES]
            zero = jnp.zeros_like(kd)
            k_half.append((jnp.where(low_half, kd, zero), jnp.where(low_half, zero, kd)))
            v_half.append((jnp.where(low_half, vd, zero), jnp.where(low_half, zero, vd)))
        scores = []
        for hq in range(GQA_Q_HEADS):
            pair, par, hk = hq // 2, hq % 2, hq // GQA_GROUP
            qp = q_ref[0, pl.ds(row0, wb), pair * LANES:(pair + 1) * LANES]
            scores.append(_dot_nt(qp, k_half[hk][par]) + (slopes[hq] * LOG2E) * nd)
        probs, rdenoms = [], []
        for hq, s in enumerate(scores):
            sink = sink_ref[hq] * LOG2E
            m = jnp.maximum(jnp.max(s, axis=-1, keepdims=True), sink)
            e = jnp.exp2(s - m)
            rdenoms.append(1.0 / (jnp.sum(e, axis=-1, keepdims=True) + jnp.exp2(sink - m)))
            probs.append(e.astype(BF16))
        outs = []
        for pair in range(GQA_Q_HEADS // 2):
            hk = (2 * pair) // GQA_GROUP
            even, odd = 2 * pair, 2 * pair + 1
            outs.append(_dot(probs[even], v_half[hk][0]) * rdenoms[even]
                        + _dot(probs[odd], v_half[hk][1]) * rdenoms[odd])
        o_ref[0, pl.ds(row0, wb), :] = jnp.concatenate(outs, axis=-1).astype(BF16)
        return carry

    lax.fori_loop(0, nq, body, 0)


def _gqa(qkv, pos_blocks, sink, slopes):
    b, s, _ = qkv.shape
    wb = WIN_BLOCK
    nb = s // wb
    nq = GQA_NQ
    r = nq * wb
    prev = lambda j: jnp.maximum(j * nq - 1, 0)
    nxt = lambda j: jnp.minimum((j + 1) * nq, nb - 1)
    edge = lambda lane_blk, f: pl.BlockSpec((1, wb, 2 * LANES), lambda bi, j: (bi, f(j), lane_blk))
    own = lambda lane_blk: pl.BlockSpec((1, r, 2 * LANES), lambda bi, j: (bi, j, lane_blk))
    pedge = lambda f: pl.BlockSpec((1, 1, wb), lambda bi, j: (bi * nb + f(j), 0, 0))
    qw = GQA_Q_HEADS * GQA_HEAD_DIM
    return pl.pallas_call(
        functools.partial(_gqa_kernel, seq=s, nq=nq, slopes=slopes),
        grid=(b, nb // nq),
        in_specs=[pl.BlockSpec(memory_space=pltpu.SMEM),
                  pl.BlockSpec((1, r, qw), lambda bi, j: (bi, j, 0)),
                  edge(2, prev), own(2), edge(2, nxt),
                  edge(3, prev), own(3), edge(3, nxt),
                  pedge(prev),
                  pl.BlockSpec((nq, 1, wb), lambda bi, j: (bi * (nb // nq) + j, 0, 0)),
                  pedge(nxt)],
        out_specs=pl.BlockSpec((1, r, qw), lambda bi, j: (bi, j, 0)),
        out_shape=jax.ShapeDtypeStruct((b, s, qw), BF16),
        scratch_shapes=[pltpu.VMEM((r + 2 * wb, 2 * LANES), BF16),
                        pltpu.VMEM((r + 2 * wb, 2 * LANES), BF16),
                        pltpu.VMEM((nq + 2, 1, wb), F32)],
        compiler_params=_params(2),
        name="gqa_window",
    )(sink, qkv, qkv, qkv, qkv, qkv, qkv, qkv, pos_blocks, pos_blocks, pos_blocks)


def _token_kernel(x_ref, oa_ref, ob_ref, oc_ref, gate_ref, p_ref,
                  wa_ref, wb_ref, wc_ref, wo_ref, g_mix_ref,
                  g_pre_ref, wg_ref, wu_ref, wd_ref, g_post_ref,
                  wp_ref, wpg_ref, g_ple_ref, out_ref):
    d = D_MODEL
    merged = (gate_ref[:, 0:d].astype(F32) * _dot(oa_ref[...], wa_ref[...])
              + gate_ref[:, d:2 * d].astype(F32) * _dot(ob_ref[...], wb_ref[...])
              + gate_ref[:, 2 * d:3 * d].astype(F32) * _dot(oc_ref[...], wc_ref[...]))
    x = x_ref[...] + _rms(_dot(merged.astype(BF16), wo_ref[...]), g_mix_ref[...])

    h = _rms(x, g_pre_ref[...]).astype(BF16)
    ff = jnp.zeros(x.shape, F32)
    for c in range(FFN_DIM // FFN_CHUNK):
        sl = slice(c * FFN_CHUNK, (c + 1) * FFN_CHUNK)
        a = jax.nn.silu(_dot(h, wg_ref[:, sl])) * _dot(h, wu_ref[:, sl])
        ff = ff + _dot(a.astype(BF16), wd_ref[sl, :])
    x = x + _rms(ff, g_post_ref[...])

    e = _dot(p_ref[...].astype(BF16), wp_ref[...]) * jax.nn.sigmoid(
        _dot(x.astype(BF16), wpg_ref[...]))
    out_ref[...] = x + _rms(e, g_ple_ref[...])


def _token_tail(x2, oa, ob, oc, gates, p2, wa, wb, wc, wo, g_mix,
                g_pre, wg, wu, wd, g_post, wp, wpg, g_ple):
    t = x2.shape[0]
    tm = TOKEN_TILE
    d = D_MODEL
    row = lambda n: pl.BlockSpec((tm, n), lambda i: (i, 0))
    vec = _const_spec((1, d))
    return pl.pallas_call(
        _token_kernel,
        grid=(t // tm,),
        in_specs=[row(d), row(oa.shape[1]), row(d), row(oc.shape[1]), row(3 * d), row(PLE_DIM),
                  _const_spec(wa.shape), _const_spec(wb.shape), _const_spec(wc.shape),
                  _const_spec(wo.shape), vec,
                  vec, _const_spec(wg.shape), _const_spec(wu.shape), _const_spec(wd.shape), vec,
                  _const_spec(wp.shape), _const_spec(wpg.shape), vec],
        out_specs=row(d),
        out_shape=jax.ShapeDtypeStruct((t, d), F32),
        compiler_params=_params(1),
        name="token_tail",
    )(x2, oa, ob, oc, gates, p2, wa, wb, wc, wo, g_mix, g_pre, wg, wu, wd, g_post, wp, wpg, g_ple)


def _cos_sin(num, den):
    ang = (num % den).astype(F32) * (2.0 * math.pi / den)
    return jnp.cos(ang), jnp.sin(ang)


def _dft_tables(seq):
    gd = FNET_GROUP_DIM
    jk = jnp.arange(gd, dtype=jnp.int32)
    c, s = _cos_sin(jk[:, None] * jk[None, :], gd)
    chan = (jnp.concatenate([c, -s], axis=1) * gd ** -0.5).astype(BF16)

    n1, n2 = FFT_N1, seq // FFT_N1
    a = jnp.arange(n1, dtype=jnp.int32)
    c1, s1 = _cos_sin(a[:, None] * a[None, :], n1)
    m1 = (jnp.concatenate([jnp.concatenate([c1, s1], axis=1),
                           jnp.concatenate([-s1, c1], axis=1)], axis=0) * n1 ** -0.5).astype(BF16)

    k1 = jnp.arange(n1, dtype=jnp.int32)[:, None, None]
    k2 = jnp.arange(n2, dtype=jnp.int32)[None, :, None]
    nn = jnp.arange(n2, dtype=jnp.int32)[None, None, :]
    cd, sd = _cos_sin(nn * (k1 + n1 * k2), seq)
    dmat = (jnp.concatenate([cd, sd], axis=2) * n2 ** -0.5).astype(BF16)
    return chan, m1, dmat


def _head_blocks(parts):
    rows, heads = parts[0].shape[:2]
    used = sum(p.shape[2] for p in parts)
    pad = jnp.zeros((rows, heads, HEAD_PAD - used), parts[0].dtype)
    return jnp.concatenate(list(parts) + [pad], axis=2).reshape(rows, heads * HEAD_PAD)


def _prep_layer(i, w_in, w_uq, w_ukv):
    wi = w_in[i]
    d = D_MODEL
    qc = wi[:, 544:1056] * ((GQA_HEAD_DIM ** -0.5) * LOG2E)
    dup = lambda w: jnp.concatenate(
        [w[:, :GQA_HEAD_DIM], w[:, :GQA_HEAD_DIM], w[:, GQA_HEAD_DIM:], w[:, GQA_HEAD_DIM:]], axis=1)
    zcols = lambda n: jnp.zeros((d, n), wi.dtype)
    w_in_p = jnp.concatenate(
        [wi[:, :512], zcols(MLA_NOPE), wi[:, 512:544], zcols(HEAD_PAD - MLA_QK), qc,
         dup(wi[:, 1056:1184]), dup(wi[:, 1184:1312]), wi[:, 1312:]], axis=1).astype(BF16)

    uq = w_uq[i].reshape(MLA_Q_LORA, MLA_HEADS, MLA_QK) * ((MLA_QK ** -0.5) * LOG2E)
    wq = _head_blocks([uq]).astype(BF16)

    ukv = w_ukv[i].reshape(MLA_KV_LORA, MLA_HEADS, MLA_NOPE + MLA_V)
    wk = _head_blocks([ukv[..., :MLA_NOPE]]).astype(BF16)
    wv = _head_blocks([ukv[..., MLA_NOPE:]]).astype(BF16)
    return w_in_p, wq, wk, wv


def kernel(x, p, positions, norm_mix_pre, w_in, mla_q_norm, w_uq, mla_kv_norm, w_ukv, gqa_sink,
           w_branch_a, w_branch_b, w_branch_c, w_out, norm_mix_post, norm_ffn_pre, w_ffn_gate,
           w_ffn_up, w_ffn_down, norm_ffn_post, w_ple_proj, w_ple_gate, norm_ple):
    b, s, d = x.shape
    depth = w_in.shape[0]
    t = b * s
    n1, n2 = FFT_N1, s // FFT_N1
    assert d == D_MODEL and t % TOKEN_TILE == 0 and s % FLASH_TQ == 0 and s % FLASH_TK == 0
    assert (n2 * d) % FFT_A_COLS == 0 and n2 % 16 == 0 and (s // WIN_BLOCK) % GQA_NQ == 0

    chan, m1, dmat = _dft_tables(s)
    half = MLA_ROPE // 2
    inv_freq = ROPE_THETA ** (-jnp.arange(half, dtype=F32) / half)
    invf_lane = jnp.zeros((1, LANES), F32)
    invf_lane = invf_lane.at[0, MLA_NOPE:MLA_NOPE + half].set(inv_freq)
    invf_lane = invf_lane.at[0, MLA_NOPE + half:MLA_NOPE + MLA_ROPE].set(inv_freq)
    posf = positions.reshape(t, 1).astype(F32)
    pos_blocks = positions.reshape(b * (s // WIN_BLOCK), 1, WIN_BLOCK)
    vbias = jnp.zeros((MLA_HEADS, HEAD_PAD), F32).at[:, MLA_V].set(1.0).reshape(1, -1)
    slopes = tuple(float(v) for v in
                   2.0 ** (-8.0 * (np.arange(GQA_Q_HEADS, dtype=np.float32) + 1.0) / GQA_Q_HEADS))
    r1 = lambda v: v.reshape(1, -1)
    (wa_all, wb_all, wc_all, wo_all, wg_all, wu_all, wd_all, wp_all, wpg_all) = [
        w.astype(BF16) for w in (w_branch_a, w_branch_b, w_branch_c, w_out, w_ffn_gate, w_ffn_up,
                                 w_ffn_down, w_ple_proj, w_ple_gate)]

    rope_cos, rope_sin = _rope_table(posf, invf_lane)

    x2 = x.reshape(t, d)
    for i in range(depth):
        w_in_p, wq, wk, wv = _prep_layer(i, w_in, w_uq, w_ukv)
        cq, ckv, qkv, gates, wr, wi = _inproj(x2, r1(norm_mix_pre[i]), w_in_p, chan)

        q, k, v = _mla_proj(cq, ckv, rope_cos, rope_sin, r1(mla_q_norm[i]), r1(mla_kv_norm[i]),
                            wq, wk, wv, vbias)
        hw = MLA_HEADS * HEAD_PAD
        o_a = _flash(q.reshape(b, s, hw), k.reshape(b, s, hw), v.reshape(b, s, hw))

        ar, ai = _fft_a(wr.reshape(b, n1, n2 * d), wi.reshape(b, n1, n2 * d), m1)
        o_b = _fft_b(ar.reshape(b, n1, n2, d), ai.reshape(b, n1, n2, d), dmat)

        o_c = _gqa(qkv.reshape(b, s, -1), pos_blocks, gqa_sink[i], slopes)

        x2 = _token_tail(
            x2, o_a.reshape(t, -1), o_b.reshape(t, d), o_c.reshape(t, -1), gates,
            p[i].reshape(t, PLE_DIM),
            wa_all[i], wb_all[i], wc_all[i], wo_all[i],
            r1(norm_mix_post[i]), r1(norm_ffn_pre[i]), wg_all[i], wu_all[i],
            wd_all[i], r1(norm_ffn_post[i]), wp_all[i], wpg_all[i],
            r1(norm_ple[i]))
    return x2.reshape(b, s, d)
```

```python
import functools
import math

import numpy as np
import jax
import jax.numpy as jnp
from jax import lax
from jax.experimental import pallas as pl
from jax.experimental.pallas import tpu as pltpu

F32 = jnp.float32
BF16 = jnp.bfloat16

D_MODEL = 1024
PLE_DIM = 256
MLA_HEADS = 8
MLA_NOPE = 64
MLA_ROPE = 32
MLA_V = 64
MLA_QK = MLA_NOPE + MLA_ROPE
MLA_Q_LORA = 384
MLA_KV_LORA = 128
ROPE_THETA = 10000.0
FNET_GROUPS = 4
FNET_GROUP_DIM = D_MODEL // FNET_GROUPS
GQA_Q_HEADS = 8
GQA_KV_HEADS = 2
GQA_GROUP = GQA_Q_HEADS // GQA_KV_HEADS
GQA_HEAD_DIM = 64
WINDOW = 128
WIN_BLOCK = 128
FFN_DIM = 2816
RMS_EPS = 1e-6

LANES = 128
HEAD_PAD = 128
FFT_N1 = 128
VMEM_LIMIT = 56 * 1024 * 1024

TOKEN_TILE = 512
FLASH_TQ = 1024
FLASH_TK = 1024
GQA_NQ = 8
FFT_A_N2_BLOCK = 16
FFT_A_COLS = 512
FFT_B_ROWS = 16
FFN_CHUNK = 256
LOG2E = 1.4426950408889634

SEG_CQ = (0, 384)
SEG_CKV = (384, 640)
SEG_QKV = (640, 1664)
SEG_GATE = (1664, 4736)
IN_COLS_PAD = 4736


def _params(n_parallel_axes):
    return pltpu.CompilerParams(
        dimension_semantics=("arbitrary",) * n_parallel_axes,
        vmem_limit_bytes=VMEM_LIMIT)


def _const_spec(shape):
    zeros = (0,) * len(shape)
    return pl.BlockSpec(shape, lambda *_: zeros, pipeline_mode=pl.Buffered(1))


def _rms(x, gain):
    ms = jnp.mean(x * x, axis=-1, keepdims=True)
    return x * lax.rsqrt(ms + RMS_EPS) * gain


def _dot(a, b):
    return jnp.dot(a, b, preferred_element_type=F32)


def _dot_nt(a, b):
    return lax.dot_general(a, b, (((1,), (1,)), ((), ())), preferred_element_type=F32)


def _inproj_kernel(x_ref, gain_ref, w_ref, dft_ref,
                   cq_ref, ckv_ref, qkv_ref, gate_ref, wr_ref, wi_ref):
    h = _rms(x_ref[...], gain_ref[...]).astype(BF16)
    cq_ref[...] = _dot(h, w_ref[:, SEG_CQ[0]:SEG_CQ[1]]).astype(BF16)
    ckv_ref[...] = _dot(h, w_ref[:, SEG_CKV[0]:SEG_CKV[1]]).astype(BF16)
    qkv_ref[...] = _dot(h, w_ref[:, SEG_QKV[0]:SEG_QKV[1]]).astype(BF16)
    for c in range(3):
        lo = SEG_GATE[0] + c * D_MODEL
        g = _dot(h, w_ref[:, lo:lo + D_MODEL])
        gate_ref[:, c * D_MODEL:(c + 1) * D_MODEL] = jax.nn.sigmoid(g).astype(BF16)
    gd = FNET_GROUP_DIM
    for g in range(FNET_GROUPS):
        yz = _dot(h[:, g * gd:(g + 1) * gd], dft_ref[...])
        wr_ref[:, g * gd:(g + 1) * gd] = yz[:, :gd].astype(BF16)
        wi_ref[:, g * gd:(g + 1) * gd] = yz[:, gd:].astype(BF16)


def _inproj(x2, gain, w_in_p, dft_c):
    t = x2.shape[0]
    tm = TOKEN_TILE
    row = lambda n: pl.BlockSpec((tm, n), lambda i: (i, 0))
    outs = [(t, 384), (t, 256), (t, SEG_QKV[1] - SEG_QKV[0]), (t, 3 * D_MODEL), (t, D_MODEL),
            (t, D_MODEL)]
    return pl.pallas_call(
        _inproj_kernel,
        grid=(t // tm,),
        in_specs=[row(D_MODEL), _const_spec((1, D_MODEL)),
                  _const_spec((D_MODEL, IN_COLS_PAD)),
                  _const_spec((FNET_GROUP_DIM, 2 * FNET_GROUP_DIM))],
        out_specs=[row(n) for _, n in outs],
        out_shape=[jax.ShapeDtypeStruct(s, BF16) for s in outs],
        compiler_params=_params(1),
        name="inproj",
    )(x2, gain, w_in_p, dft_c)


def _rope_table_kernel(pos_ref, invf_ref, cos_ref, sin_ref):
    ang = pos_ref[...] * invf_ref[...]
    cos_ref[...] = jnp.cos(ang)
    sin_ref[...] = jnp.sin(ang)


def _rope_table(posf, invf_lane):
    t = posf.shape[0]
    tm = TOKEN_TILE
    row = lambda n: pl.BlockSpec((tm, n), lambda i: (i, 0))
    return pl.pallas_call(
        _rope_table_kernel,
        grid=(t // tm,),
        in_specs=[row(1), _const_spec((1, LANES))],
        out_specs=[row(LANES)] * 2,
        out_shape=[jax.ShapeDtypeStruct((t, LANES), F32)] * 2,
        compiler_params=_params(1),
        name="rope_table",
    )(posf, invf_lane)


def _mla_proj_kernel(cq_ref, ckv_ref, cos_ref, sin_ref, qn_ref, kvn_ref,
                     wq_ref, wk_ref, wv_ref, vb_ref, q_ref, k_ref, v_ref):
    cqn = _rms(cq_ref[...].astype(F32), qn_ref[...]).astype(BF16)
    ckvn = _rms(ckv_ref[:, :MLA_KV_LORA].astype(F32), kvn_ref[...]).astype(BF16)
    cos = cos_ref[...]
    sin = sin_ref[...]
    half = MLA_ROPE // 2
    lane = lax.broadcasted_iota(jnp.int32, cos.shape, 1)
    second = lane >= MLA_NOPE + half
    sin_up = jnp.where(second, sin, 0.0)
    sin_dn = jnp.where(second, 0.0, -sin)

    def rope(x):
        return (x * cos + pltpu.roll(x, half, 1) * sin_up
                + pltpu.roll(x, HEAD_PAD - half, 1) * sin_dn)

    kr = rope(ckv_ref[:, MLA_KV_LORA:].astype(F32))
    for h in range(MLA_HEADS):
        sl = slice(h * HEAD_PAD, (h + 1) * HEAD_PAD)
        q_ref[:, sl] = rope(_dot(cqn, wq_ref[:, sl])).astype(BF16)
        k_ref[:, sl] = (_dot(ckvn, wk_ref[:, sl]) + kr).astype(BF16)
        v_ref[:, sl] = (_dot(ckvn, wv_ref[:, sl]) + vb_ref[:, sl]).astype(BF16)


def _mla_proj(cq, ckv, cos, sin, qn, kvn, wq, wk, wv, vb):
    t = cq.shape[0]
    tm = TOKEN_TILE
    hw = MLA_HEADS * HEAD_PAD
    row = lambda n: pl.BlockSpec((tm, n), lambda i: (i, 0))
    return pl.pallas_call(
        _mla_proj_kernel,
        grid=(t // tm,),
        in_specs=[row(MLA_Q_LORA), row(256), row(LANES), row(LANES),
                  _const_spec((1, MLA_Q_LORA)), _const_spec((1, MLA_KV_LORA)),
                  _const_spec((MLA_Q_LORA, hw)), _const_spec((MLA_KV_LORA, hw)),
                  _const_spec((MLA_KV_LORA, hw)), _const_spec((1, hw))],
        out_specs=[row(hw)] * 3,
        out_shape=[jax.ShapeDtypeStruct((t, hw), BF16)] * 3,
        compiler_params=_params(1),
        name="mla_proj",
    )(cq, ckv, cos, sin, qn, kvn, wq, wk, wv, vb)


def _flash_kernel(q_ref, k_ref, v_ref, o_ref, *, tk, nk):
    tq = q_ref.shape[1]
    heads = [slice(h * HEAD_PAD, (h + 1) * HEAD_PAD) for h in range(2)]
    qs = [q_ref[0, :, sl] for sl in heads]

    def scores(i):
        return [_dot_nt(q, k_ref[0, i * tk:(i + 1) * tk, sl]) for q, sl in zip(qs, heads)]

    state = [(jnp.full((tq, 1), -jnp.inf, F32), jnp.zeros((tq, HEAD_PAD), F32))] * 2
    ss = scores(0)
    for i in range(nk):
        ss_next = scores(i + 1) if i + 1 < nk else None
        new = []
        for (m, acc), s, sl in zip(state, ss, heads):
            m_new = jnp.maximum(m, jnp.max(s, axis=-1, keepdims=True))
            p = jnp.exp2((s - m_new).astype(BF16))
            alpha = jnp.exp2(m - m_new)
            v = v_ref[0, i * tk:(i + 1) * tk, sl]
            new.append((m_new, alpha * acc + _dot(p, v)))
        state, ss = new, ss_next
    outs = [acc[:, :MLA_V] / acc[:, MLA_V:MLA_V + 1] for _, acc in state]
    o_ref[0] = jnp.concatenate(outs, axis=-1).astype(BF16)


def _flash(q, k, v):
    b, s, hw = q.shape
    tq, tk = FLASH_TQ, FLASH_TK
    pair = 2 * HEAD_PAD
    return pl.pallas_call(
        functools.partial(_flash_kernel, tk=tk, nk=s // tk),
        grid=(b, MLA_HEADS // 2, s // tq),
        in_specs=[pl.BlockSpec((1, tq, pair), lambda bi, hi, qi: (bi, qi, hi)),
                  pl.BlockSpec((1, s, pair), lambda bi, hi, qi: (bi, 0, hi)),
                  pl.BlockSpec((1, s, pair), lambda bi, hi, qi: (bi, 0, hi))],
        out_specs=pl.BlockSpec((1, tq, 2 * MLA_V), lambda bi, hi, qi: (bi, qi, hi)),
        out_shape=jax.ShapeDtypeStruct((b, s, MLA_HEADS * MLA_V), BF16),
        compiler_params=_params(3),
        name="mla_flash",
    )(q, k, v)


def _fft_a_kernel(xr_ref, xi_ref, m1_ref, ar_ref, ai_ref):
    xr = jnp.swapaxes(xr_ref[0], 0, 1)
    xi = jnp.swapaxes(xi_ref[0], 0, 1)
    out_r, out_i = [], []
    for j in range(xr.shape[0]):
        a = _dot(m1_ref[...], jnp.concatenate([xr[j], xi[j]], axis=0))
        out_r.append(a[:FFT_N1].astype(BF16))
        out_i.append(a[FFT_N1:].astype(BF16))
    ar_ref[0] = jnp.swapaxes(jnp.stack(out_r), 0, 1)
    ai_ref[0] = jnp.swapaxes(jnp.stack(out_i), 0, 1)


def _fft_a(wr, wi, m1):
    b, n1, n2, c = wr.shape
    nb2, tc = FFT_A_N2_BLOCK, FFT_A_COLS
    blk = pl.BlockSpec((1, n1, nb2, tc), lambda bi, ji, ci: (bi, 0, ji, ci))
    return pl.pallas_call(
        _fft_a_kernel,
        grid=(b, n2 // nb2, c // tc),
        in_specs=[blk, blk, _const_spec((2 * n1, 2 * n1))],
        out_specs=[blk, blk],
        out_shape=[jax.ShapeDtypeStruct((b, n1, n2, c), BF16)] * 2,
        compiler_params=_params(3),
        name="fft_a",
    )(wr, wi, m1)


def _fft_b_kernel(ar_ref, ai_ref, d_ref, o_ref, *, kb):
    res = []
    for j in range(kb):
        a = jnp.concatenate([ar_ref[0, j], ai_ref[0, j]], axis=0)
        res.append(_dot(d_ref[j], a).astype(BF16))
    o_ref[0] = jnp.swapaxes(jnp.stack(res), 0, 1)


def _fft_b(ar, ai, dmat):
    b, n1, n2, c = ar.shape
    kb = FFT_B_ROWS
    blk = pl.BlockSpec((1, kb, n2, c), lambda bi, ki: (bi, ki, 0, 0))
    return pl.pallas_call(
        functools.partial(_fft_b_kernel, kb=kb),
        grid=(b, n1 // kb),
        in_specs=[blk, blk, pl.BlockSpec((kb, n2, 2 * n2), lambda bi, ki: (ki, 0, 0))],
        out_specs=pl.BlockSpec((1, n2, kb, c), lambda bi, ki: (bi, 0, ki, 0)),
        out_shape=jax.ShapeDtypeStruct((b, n2, n1, c), BF16),
        compiler_params=_params(2),
        name="fft_b",
    )(ar, ai, dmat)


def _gqa_kernel(sink_ref, q_ref, kp_ref, kc_ref, kn_ref, vp_ref, vc_ref, vn_ref,
                pp_ref, pc_ref, pn_ref, o_ref, k_scr, v_scr, p_scr, *, seq, nq, slopes):
    j = pl.program_id(1)
    wb = WIN_BLOCK
    r = nq * wb
    k_scr[0:wb] = kp_ref[0]
    k_scr[wb:wb + r] = kc_ref[0]
    k_scr[wb + r:] = kn_ref[0]
    v_scr[0:wb] = vp_ref[0]
    v_scr[wb:wb + r] = vc_ref[0]
    v_scr[wb + r:] = vn_ref[0]
    p_scr[0] = pp_ref[0].astype(F32)
    p_scr[1:nq + 1] = pc_ref[...].astype(F32)
    p_scr[nq + 1] = pn_ref[0].astype(F32)

    kj = lax.broadcasted_iota(jnp.int32, (wb, 3 * wb), 1)
    qi = lax.broadcasted_iota(jnp.int32, (wb, 3 * wb), 0)
    band = jnp.abs(kj - wb - qi) <= WINDOW
    low_half = lax.broadcasted_iota(jnp.int32, (3 * wb, LANES), 1) < GQA_HEAD_DIM

    def body(a, carry):
        row0 = pl.multiple_of(a * wb, wb)
        kw = k_scr[pl.ds(row0, 3 * wb), :]
        vw = v_scr[pl.ds(row0, 3 * wb), :]
        pc = p_scr[a + 1]
        pq = jnp.broadcast_to(pc, (wb, wb)).T
        dist = jnp.concatenate(
            [jnp.abs(pq - p_scr[a]), jnp.abs(pq - pc), jnp.abs(pq - p_scr[a + 2])], axis=1)
        key_idx = (j * nq + a) * wb - wb + kj
        mask = band & (key_idx >= 0) & (key_idx < seq)
        nd = jnp.where(mask, -dist, -jnp.inf)
        k_half, v_half = [], []
        for hk in range(GQA_KV_HEADS):
            kd = kw[:, hk * LANES:(hk + 1) * LANES]
            vd = vw[:, hk * LANES:(hk + 1) * LANES]
            zero = jnp.zeros_like(kd)
            k_half.append((jnp.where(low_half, kd, zero), jnp.where(low_half, zero, kd)))
            v_half.append((jnp.where(low_half, vd, zero), jnp.where(low_half, zero, vd)))
        scores = []
        for hq in range(GQA_Q_HEADS):
            pair, par, hk = hq // 2, hq % 2, hq // GQA_GROUP
            qp = q_ref[0, pl.ds(row0, wb), pair * LANES:(pair + 1) * LANES]
            scores.append(_dot_nt(qp, k_half[hk][par]) + (slopes[hq] * LOG2E) * nd)
        probs, rdenoms = [], []
        for hq, s in enumerate(scores):
            sink = sink_ref[hq] * LOG2E
            m = jnp.maximum(jnp.max(s, axis=-1, keepdims=True), sink)
            e = jnp.exp2(s - m)
            rdenoms.append(1.0 / (jnp.sum(e, axis=-1, keepdims=True) + jnp.exp2(sink - m)))
            probs.append(e.astype(BF16))
        outs = []
        for pair in range(GQA_Q_HEADS // 2):
            hk = (2 * pair) // GQA_GROUP
            even, odd = 2 * pair, 2 * pair + 1
            outs.append(_dot(probs[even], v_half[hk][0]) * rdenoms[even]
                        + _dot(probs[odd], v_half[hk][1]) * rdenoms[odd])
        o_ref[0, pl.ds(row0, wb), :] = jnp.concatenate(outs, axis=-1).astype(BF16)
        return carry

    lax.fori_loop(0, nq, body, 0)


def _gqa(qkv, pos_blocks, sink, slopes):
    b, s, _ = qkv.shape
    wb = WIN_BLOCK
    nb = s // wb
    nq = GQA_NQ
    r = nq * wb
    prev = lambda j: jnp.maximum(j * nq - 1, 0)
    nxt = lambda j: jnp.minimum((j + 1) * nq, nb - 1)
    edge = lambda lane_blk, f: pl.BlockSpec((1, wb, 2 * LANES), lambda bi, j: (bi, f(j), lane_blk))
    own = lambda lane_blk: pl.BlockSpec((1, r, 2 * LANES), lambda bi, j: (bi, j, lane_blk))
    pedge = lambda f: pl.BlockSpec((1, 1, wb), lambda bi, j: (bi * nb + f(j), 0, 0))
    qw = GQA_Q_HEADS * GQA_HEAD_DIM
    return pl.pallas_call(
        functools.partial(_gqa_kernel, seq=s, nq=nq, slopes=slopes),
        grid=(b, nb // nq),
        in_specs=[pl.BlockSpec(memory_space=pltpu.SMEM),
                  pl.BlockSpec((1, r, qw), lambda bi, j: (bi, j, 0)),
                  edge(2, prev), own(2), edge(2, nxt),
                  edge(3, prev), own(3), edge(3, nxt),
                  pedge(prev),
                  pl.BlockSpec((nq, 1, wb), lambda bi, j: (bi * (nb // nq) + j, 0, 0)),
                  pedge(nxt)],
        out_specs=pl.BlockSpec((1, r, qw), lambda bi, j: (bi, j, 0)),
        out_shape=jax.ShapeDtypeStruct((b, s, qw), BF16),
        scratch_shapes=[pltpu.VMEM((r + 2 * wb, 2 * LANES), BF16),
                        pltpu.VMEM((r + 2 * wb, 2 * LANES), BF16),
                        pltpu.VMEM((nq + 2, 1, wb), F32)],
        compiler_params=_params(2),
        name="gqa_window",
    )(sink, qkv, qkv, qkv, qkv, qkv, qkv, qkv, pos_blocks, pos_blocks, pos_blocks)


def _token_kernel(x_ref, oa_ref, ob_ref, oc_ref, gate_ref, p_ref,
                  wa_ref, wb_ref, wc_ref, wo_ref, g_mix_ref,
                  g_pre_ref, wg_ref, wu_ref, wd_ref, g_post_ref,
                  wp_ref, wpg_ref, g_ple_ref, out_ref):
    d = D_MODEL
    merged = (gate_ref[:, 0:d].astype(F32) * _dot(oa_ref[...], wa_ref[...])
              + gate_ref[:, d:2 * d].astype(F32) * _dot(ob_ref[...], wb_ref[...])
              + gate_ref[:, 2 * d:3 * d].astype(F32) * _dot(oc_ref[...], wc_ref[...]))
    x = x_ref[...] + _rms(_dot(merged.astype(BF16), wo_ref[...]), g_mix_ref[...])

    h = _rms(x, g_pre_ref[...]).astype(BF16)
    ff = jnp.zeros(x.shape, F32)
    for c in range(FFN_DIM // FFN_CHUNK):
        sl = slice(c * FFN_CHUNK, (c + 1) * FFN_CHUNK)
        a = jax.nn.silu(_dot(h, wg_ref[:, sl])) * _dot(h, wu_ref[:, sl])
        ff = ff + _dot(a.astype(BF16), wd_ref[sl, :])
    x = x + _rms(ff, g_post_ref[...])

    e = _dot(p_ref[...].astype(BF16), wp_ref[...]) * jax.nn.sigmoid(
        _dot(x.astype(BF16), wpg_ref[...]))
    out_ref[...] = x + _rms(e, g_ple_ref[...])


def _token_tail(x2, oa, ob, oc, gates, p2, wa, wb, wc, wo, g_mix,
                g_pre, wg, wu, wd, g_post, wp, wpg, g_ple):
    t = x2.shape[0]
    tm = TOKEN_TILE
    d = D_MODEL
    row = lambda n: pl.BlockSpec((tm, n), lambda i: (i, 0))
    vec = _const_spec((1, d))
    return pl.pallas_call(
        _token_kernel,
        grid=(t // tm,),
        in_specs=[row(d), row(oa.shape[1]), row(d), row(oc.shape[1]), row(3 * d), row(PLE_DIM),
                  _const_spec(wa.shape), _const_spec(wb.shape), _const_spec(wc.shape),
                  _const_spec(wo.shape), vec,
                  vec, _const_spec(wg.shape), _const_spec(wu.shape), _const_spec(wd.shape), vec,
                  _const_spec(wp.shape), _const_spec(wpg.shape), vec],
        out_specs=row(d),
        out_shape=jax.ShapeDtypeStruct((t, d), F32),
        compiler_params=_params(1),
        name="token_tail",
    )(x2, oa, ob, oc, gates, p2, wa, wb, wc, wo, g_mix, g_pre, wg, wu, wd, g_post, wp, wpg, g_ple)


def _cos_sin(num, den):
    ang = (num % den).astype(F32) * (2.0 * math.pi / den)
    return jnp.cos(ang), jnp.sin(ang)


def _dft_tables(seq):
    gd = FNET_GROUP_DIM
    jk = jnp.arange(gd, dtype=jnp.int32)
    c, s = _cos_sin(jk[:, None] * jk[None, :], gd)
    chan = (jnp.concatenate([c, -s], axis=1) * gd ** -0.5).astype(BF16)

    n1, n2 = FFT_N1, seq // FFT_N1
    a = jnp.arange(n1, dtype=jnp.int32)
    c1, s1 = _cos_sin(a[:, None] * a[None, :], n1)
    m1 = (jnp.concatenate([jnp.concatenate([c1, s1], axis=1),
                           jnp.concatenate([-s1, c1], axis=1)], axis=0) * n1 ** -0.5).astype(BF16)

    k1 = jnp.arange(n1, dtype=jnp.int32)[:, None, None]
    k2 = jnp.arange(n2, dtype=jnp.int32)[None, :, None]
    nn = jnp.arange(n2, dtype=jnp.int32)[None, None, :]
    cd, sd = _cos_sin(nn * (k1 + n1 * k2), seq)
    dmat = (jnp.concatenate([cd, sd], axis=2) * n2 ** -0.5).astype(BF16)
    return chan, m1, dmat


def _head_blocks(parts):
    rows, heads = parts[0].shape[:2]
    used = sum(p.shape[2] for p in parts)
    pad = jnp.zeros((rows, heads, HEAD_PAD - used), parts[0].dtype)
    return jnp.concatenate(list(parts) + [pad], axis=2).reshape(rows, heads * HEAD_PAD)


def _prep_layer(i, w_in, w_uq, w_ukv):
    wi = w_in[i]
    d = D_MODEL
    qc = wi[:, 544:1056] * ((GQA_HEAD_DIM ** -0.5) * LOG2E)
    dup = lambda w: jnp.concatenate(
        [w[:, :GQA_HEAD_DIM], w[:, :GQA_HEAD_DIM], w[:, GQA_HEAD_DIM:], w[:, GQA_HEAD_DIM:]], axis=1)
    zcols = lambda n: jnp.zeros((d, n), wi.dtype)
    w_in_p = jnp.concatenate(
        [wi[:, :512], zcols(MLA_NOPE), wi[:, 512:544], zcols(HEAD_PAD - MLA_QK), qc,
         dup(wi[:, 1056:1184]), dup(wi[:, 1184:1312]), wi[:, 1312:]], axis=1).astype(BF16)

    uq = w_uq[i].reshape(MLA_Q_LORA, MLA_HEADS, MLA_QK) * ((MLA_QK ** -0.5) * LOG2E)
    wq = _head_blocks([uq]).astype(BF16)

    ukv = w_ukv[i].reshape(MLA_KV_LORA, MLA_HEADS, MLA_NOPE + MLA_V)
    wk = _head_blocks([ukv[..., :MLA_NOPE]]).astype(BF16)
    wv = _head_blocks([ukv[..., MLA_NOPE:]]).astype(BF16)
    return w_in_p, wq, wk, wv


def kernel(x, p, positions, norm_mix_pre, w_in, mla_q_norm, w_uq, mla_kv_norm, w_ukv, gqa_sink,
           w_branch_a, w_branch_b, w_branch_c, w_out, norm_mix_post, norm_ffn_pre, w_ffn_gate,
           w_ffn_up, w_ffn_down, norm_ffn_post, w_ple_proj, w_ple_gate, norm_ple):
    b, s, d = x.shape
    depth = w_in.shape[0]
    t = b * s
    n1, n2 = FFT_N1, s // FFT_N1
    assert d == D_MODEL and t % TOKEN_TILE == 0 and s % FLASH_TQ == 0 and s % FLASH_TK == 0
    assert n2 % FFT_A_N2_BLOCK == 0 and d % FFT_A_COLS == 0 and (s // WIN_BLOCK) % GQA_NQ == 0

    chan, m1, dmat = _dft_tables(s)
    half = MLA_ROPE // 2
    inv_freq = ROPE_THETA ** (-jnp.arange(half, dtype=F32) / half)
    invf_lane = jnp.zeros((1, LANES), F32)
    invf_lane = invf_lane.at[0, MLA_NOPE:MLA_NOPE + half].set(inv_freq)
    invf_lane = invf_lane.at[0, MLA_NOPE + half:MLA_NOPE + MLA_ROPE].set(inv_freq)
    posf = positions.reshape(t, 1).astype(F32)
    pos_blocks = positions.reshape(b * (s // WIN_BLOCK), 1, WIN_BLOCK)
    vbias = jnp.zeros((MLA_HEADS, HEAD_PAD), F32).at[:, MLA_V].set(1.0).reshape(1, -1)
    slopes = tuple(float(v) for v in
                   2.0 ** (-8.0 * (np.arange(GQA_Q_HEADS, dtype=np.float32) + 1.0) / GQA_Q_HEADS))
    r1 = lambda v: v.reshape(1, -1)
    (wa_all, wb_all, wc_all, wo_all, wg_all, wu_all, wd_all, wp_all, wpg_all) = [
        w.astype(BF16) for w in (w_branch_a, w_branch_b, w_branch_c, w_out, w_ffn_gate, w_ffn_up,
                                 w_ffn_down, w_ple_proj, w_ple_gate)]

    rope_cos, rope_sin = _rope_table(posf, invf_lane)

    x2 = x.reshape(t, d)
    for i in range(depth):
        w_in_p, wq, wk, wv = _prep_layer(i, w_in, w_uq, w_ukv)
        cq, ckv, qkv, gates, wr, wi = _inproj(x2, r1(norm_mix_pre[i]), w_in_p, chan)

        q, k, v = _mla_proj(cq, ckv, rope_cos, rope_sin, r1(mla_q_norm[i]), r1(mla_kv_norm[i]),
                            wq, wk, wv, vbias)
        hw = MLA_HEADS * HEAD_PAD
        o_a = _flash(q.reshape(b, s, hw), k.reshape(b, s, hw), v.reshape(b, s, hw))

        ar, ai = _fft_a(wr.reshape(b, n1, n2, d), wi.reshape(b, n1, n2, d), m1)
        o_b = _fft_b(ar, ai, dmat)

        o_c = _gqa(qkv.reshape(b, s, -1), pos_blocks, gqa_sink[i], slopes)

        x2 = _token_tail(
            x2, o_a.reshape(t, -1), o_b.reshape(t, d), o_c.reshape(t, -1), gates,
            p[i].reshape(t, PLE_DIM),
            wa_all[i], wb_all[i], wc_all[i], wo_all[i],
            r1(norm_mix_post[i]), r1(norm_ffn_pre[i]), wg_all[i], wu_all[i],
            wd_all[i], r1(norm_ffn_post[i]), wp_all[i], wpg_all[i],
            r1(norm_ple[i]))
    return x2.reshape(b, s, d)
```

```python
import functools
import math

import numpy as np
import jax
import jax.numpy as jnp
from jax import lax
from jax.experimental import pallas as pl
from jax.experimental.pallas import tpu as pltpu

F32 = jnp.float32
BF16 = jnp.bfloat16

D_MODEL = 1024
PLE_DIM = 256
MLA_HEADS = 8
MLA_NOPE = 64
MLA_ROPE = 32
MLA_V = 64
MLA_QK = MLA_NOPE + MLA_ROPE
MLA_Q_LORA = 384
MLA_KV_LORA = 128
ROPE_THETA = 10000.0
FNET_GROUPS = 4
FNET_GROUP_DIM = D_MODEL // FNET_GROUPS
GQA_Q_HEADS = 8
GQA_KV_HEADS = 2
GQA_GROUP = GQA_Q_HEADS // GQA_KV_HEADS
GQA_HEAD_DIM = 64
WINDOW = 128
WIN_BLOCK = 128
FFN_DIM = 2816
RMS_EPS = 1e-6

LANES = 128
HEAD_PAD = 128
FFT_N1 = 128
VMEM_LIMIT = 56 * 1024 * 1024

TOKEN_TILE = 512
MLA_PROJ_TILE = 1024
FLASH_TQ = 1024
FLASH_TK = 1024
GQA_NQ = 8
FFT_A_N2_BLOCK = 16
FFT_A_COLS = 512
FFT_B_ROWS = 16
FFN_CHUNK = 256
LOG2E = 1.4426950408889634

SEG_CQ = (0, 384)
SEG_CKV = (384, 640)
SEG_QKV = (640, 1664)
SEG_GATE = (1664, 4736)
IN_COLS_PAD = 4736


def _params(n_parallel_axes):
    return pltpu.CompilerParams(
        dimension_semantics=("arbitrary",) * n_parallel_axes,
        vmem_limit_bytes=VMEM_LIMIT)


def _const_spec(shape):
    zeros = (0,) * len(shape)
    return pl.BlockSpec(shape, lambda *_: zeros, pipeline_mode=pl.Buffered(1))


def _rms(x, gain):
    ms = jnp.mean(x * x, axis=-1, keepdims=True)
    return x * lax.rsqrt(ms + RMS_EPS) * gain


def _dot(a, b):
    return jnp.dot(a, b, preferred_element_type=F32)


def _dot_nt(a, b):
    return lax.dot_general(a, b, (((1,), (1,)), ((), ())), preferred_element_type=F32)


def _inproj_kernel(x_ref, gain_ref, w_ref, dft_ref,
                   cq_ref, ckv_ref, qkv_ref, gate_ref, wr_ref, wi_ref):
    h = _rms(x_ref[...], gain_ref[...]).astype(BF16)
    cq_ref[...] = _dot(h, w_ref[:, SEG_CQ[0]:SEG_CQ[1]]).astype(BF16)
    ckv_ref[...] = _dot(h, w_ref[:, SEG_CKV[0]:SEG_CKV[1]]).astype(BF16)
    qkv_ref[...] = _dot(h, w_ref[:, SEG_QKV[0]:SEG_QKV[1]]).astype(BF16)
    for c in range(3):
        lo = SEG_GATE[0] + c * D_MODEL
        g = _dot(h, w_ref[:, lo:lo + D_MODEL])
        gate_ref[:, c * D_MODEL:(c + 1) * D_MODEL] = jax.nn.sigmoid(g).astype(BF16)
    gd = FNET_GROUP_DIM
    for g in range(FNET_GROUPS):
        yz = _dot(h[:, g * gd:(g + 1) * gd], dft_ref[...])
        wr_ref[:, g * gd:(g + 1) * gd] = yz[:, :gd].astype(BF16)
        wi_ref[:, g * gd:(g + 1) * gd] = yz[:, gd:].astype(BF16)


def _inproj(x2, gain, w_in_p, dft_c):
    t = x2.shape[0]
    tm = TOKEN_TILE
    row = lambda n: pl.BlockSpec((tm, n), lambda i: (i, 0))
    outs = [(t, 384), (t, 256), (t, SEG_QKV[1] - SEG_QKV[0]), (t, 3 * D_MODEL), (t, D_MODEL),
            (t, D_MODEL)]
    return pl.pallas_call(
        _inproj_kernel,
        grid=(t // tm,),
        in_specs=[row(D_MODEL), _const_spec((1, D_MODEL)),
                  _const_spec((D_MODEL, IN_COLS_PAD)),
                  _const_spec((FNET_GROUP_DIM, 2 * FNET_GROUP_DIM))],
        out_specs=[row(n) for _, n in outs],
        out_shape=[jax.ShapeDtypeStruct(s, BF16) for s in outs],
        compiler_params=_params(1),
        name="inproj",
    )(x2, gain, w_in_p, dft_c)


def _rope_table_kernel(pos_ref, invf_ref, cos_ref, sin_ref):
    ang = pos_ref[...] * invf_ref[...]
    cos_ref[...] = jnp.cos(ang)
    sin_ref[...] = jnp.sin(ang)


def _rope_table(posf, invf_lane):
    t = posf.shape[0]
    tm = TOKEN_TILE
    row = lambda n: pl.BlockSpec((tm, n), lambda i: (i, 0))
    return pl.pallas_call(
        _rope_table_kernel,
        grid=(t // tm,),
        in_specs=[row(1), _const_spec((1, LANES))],
        out_specs=[row(LANES)] * 2,
        out_shape=[jax.ShapeDtypeStruct((t, LANES), F32)] * 2,
        compiler_params=_params(1),
        name="rope_table",
    )(posf, invf_lane)


def _mla_proj_kernel(cq_ref, ckv_ref, cos_ref, sin_ref, qn_ref, kvn_ref,
                     wq_ref, wk_ref, wv_ref, vb_ref, q_ref, k_ref, v_ref):
    cqn = _rms(cq_ref[...].astype(F32), qn_ref[...]).astype(BF16)
    ckvn = _rms(ckv_ref[:, :MLA_KV_LORA].astype(F32), kvn_ref[...]).astype(BF16)
    cos = cos_ref[...]
    sin = sin_ref[...]
    half = MLA_ROPE // 2
    lane = lax.broadcasted_iota(jnp.int32, cos.shape, 1)
    second = lane >= MLA_NOPE + half
    sin_up = jnp.where(second, sin, 0.0)
    sin_dn = jnp.where(second, 0.0, -sin)

    def rope(x):
        return (x * cos + pltpu.roll(x, half, 1) * sin_up
                + pltpu.roll(x, HEAD_PAD - half, 1) * sin_dn)

    kr = rope(ckv_ref[:, MLA_KV_LORA:].astype(F32))
    for h in range(MLA_HEADS):
        sl = slice(h * HEAD_PAD, (h + 1) * HEAD_PAD)
        q_ref[:, sl] = rope(_dot(cqn, wq_ref[:, sl])).astype(BF16)
        k_ref[:, sl] = (_dot(ckvn, wk_ref[:, sl]) + kr).astype(BF16)
        v_ref[:, sl] = (_dot(ckvn, wv_ref[:, sl]) + vb_ref[:, sl]).astype(BF16)


def _mla_proj(cq, ckv, cos, sin, qn, kvn, wq, wk, wv, vb):
    t = cq.shape[0]
    tm = MLA_PROJ_TILE
    hw = MLA_HEADS * HEAD_PAD
    row = lambda n: pl.BlockSpec((tm, n), lambda i: (i, 0))
    return pl.pallas_call(
        _mla_proj_kernel,
        grid=(t // tm,),
        in_specs=[row(MLA_Q_LORA), row(256), row(LANES), row(LANES),
                  _const_spec((1, MLA_Q_LORA)), _const_spec((1, MLA_KV_LORA)),
                  _const_spec((MLA_Q_LORA, hw)), _const_spec((MLA_KV_LORA, hw)),
                  _const_spec((MLA_KV_LORA, hw)), _const_spec((1, hw))],
        out_specs=[row(hw)] * 3,
        out_shape=[jax.ShapeDtypeStruct((t, hw), BF16)] * 3,
        compiler_params=_params(1),
        name="mla_proj",
    )(cq, ckv, cos, sin, qn, kvn, wq, wk, wv, vb)


def _flash_kernel(q_ref, k_ref, v_ref, o_ref, *, tk, nk):
    tq = q_ref.shape[1]
    heads = [slice(h * HEAD_PAD, (h + 1) * HEAD_PAD) for h in range(2)]
    qs = [q_ref[0, :, sl] for sl in heads]

    def scores(i):
        return [_dot_nt(q, k_ref[0, i * tk:(i + 1) * tk, sl]) for q, sl in zip(qs, heads)]

    state = [(jnp.full((tq, 1), -jnp.inf, F32), jnp.zeros((tq, HEAD_PAD), F32))] * 2
    ss = scores(0)
    for i in range(nk):
        ss_next = scores(i + 1) if i + 1 < nk else None
        new = []
        for (m, acc), s, sl in zip(state, ss, heads):
            m_new = jnp.maximum(m, jnp.max(s, axis=-1, keepdims=True))
            p = jnp.exp2((s - m_new).astype(BF16))
            alpha = jnp.exp2(m - m_new)
            v = v_ref[0, i * tk:(i + 1) * tk, sl]
            new.append((m_new, alpha * acc + _dot(p, v)))
        state, ss = new, ss_next
    outs = [acc[:, :MLA_V] / acc[:, MLA_V:MLA_V + 1] for _, acc in state]
    o_ref[0] = jnp.concatenate(outs, axis=-1).astype(BF16)


def _flash(q, k, v):
    b, s, hw = q.shape
    tq, tk = FLASH_TQ, FLASH_TK
    pair = 2 * HEAD_PAD
    return pl.pallas_call(
        functools.partial(_flash_kernel, tk=tk, nk=s // tk),
        grid=(b, MLA_HEADS // 2, s // tq),
        in_specs=[pl.BlockSpec((1, tq, pair), lambda bi, hi, qi: (bi, qi, hi)),
                  pl.BlockSpec((1, s, pair), lambda bi, hi, qi: (bi, 0, hi)),
                  pl.BlockSpec((1, s, pair), lambda bi, hi, qi: (bi, 0, hi))],
        out_specs=pl.BlockSpec((1, tq, 2 * MLA_V), lambda bi, hi, qi: (bi, qi, hi)),
        out_shape=jax.ShapeDtypeStruct((b, s, MLA_HEADS * MLA_V), BF16),
        compiler_params=_params(3),
        name="mla_flash",
    )(q, k, v)


def _fft_a_kernel(xr_ref, xi_ref, m1_ref, ar_ref, ai_ref):
    xr = jnp.swapaxes(xr_ref[0], 0, 1)
    xi = jnp.swapaxes(xi_ref[0], 0, 1)
    out_r, out_i = [], []
    for j in range(xr.shape[0]):
        a = _dot(m1_ref[...], jnp.concatenate([xr[j], xi[j]], axis=0))
        out_r.append(a[:FFT_N1].astype(BF16))
        out_i.append(a[FFT_N1:].astype(BF16))
    ar_ref[0] = jnp.swapaxes(jnp.stack(out_r), 0, 1)
    ai_ref[0] = jnp.swapaxes(jnp.stack(out_i), 0, 1)


def _fft_a(wr, wi, m1):
    b, n1, n2, c = wr.shape
    nb2, tc = FFT_A_N2_BLOCK, FFT_A_COLS
    blk = pl.BlockSpec((1, n1, nb2, tc), lambda bi, ji, ci: (bi, 0, ji, ci))
    return pl.pallas_call(
        _fft_a_kernel,
        grid=(b, n2 // nb2, c // tc),
        in_specs=[blk, blk, _const_spec((2 * n1, 2 * n1))],
        out_specs=[blk, blk],
        out_shape=[jax.ShapeDtypeStruct((b, n1, n2, c), BF16)] * 2,
        compiler_params=_params(3),
        name="fft_a",
    )(wr, wi, m1)


def _fft_b_kernel(ar_ref, ai_ref, d_ref, o_ref, *, kb):
    res = []
    for j in range(kb):
        a = jnp.concatenate([ar_ref[0, j], ai_ref[0, j]], axis=0)
        res.append(_dot(d_ref[j], a).astype(BF16))
    o_ref[0] = jnp.swapaxes(jnp.stack(res), 0, 1)


def _fft_b(ar, ai, dmat):
    b, n1, n2, c = ar.shape
    kb = FFT_B_ROWS
    blk = pl.BlockSpec((1, kb, n2, c), lambda bi, ki: (bi, ki, 0, 0))
    return pl.pallas_call(
        functools.partial(_fft_b_kernel, kb=kb),
        grid=(b, n1 // kb),
        in_specs=[blk, blk, pl.BlockSpec((kb, n2, 2 * n2), lambda bi, ki: (ki, 0, 0))],
        out_specs=pl.BlockSpec((1, n2, kb, c), lambda bi, ki: (bi, 0, ki, 0)),
        out_shape=jax.ShapeDtypeStruct((b, n2, n1, c), BF16),
        compiler_params=_params(2),
        name="fft_b",
    )(ar, ai, dmat)


def _gqa_kernel(sink_ref, q_ref, kp_ref, kc_ref, kn_ref, vp_ref, vc_ref, vn_ref,
                pp_ref, pc_ref, pn_ref, o_ref, k_scr, v_scr, p_scr, *, seq, nq, slopes):
    j = pl.program_id(1)
    wb = WIN_BLOCK
    r = nq * wb
    k_scr[0:wb] = kp_ref[0]
    k_scr[wb:wb + r] = kc_ref[0]
    k_scr[wb + r:] = kn_ref[0]
    v_scr[0:wb] = vp_ref[0]
    v_scr[wb:wb + r] = vc_ref[0]
    v_scr[wb + r:] = vn_ref[0]
    p_scr[0] = pp_ref[0].astype(F32)
    p_scr[1:nq + 1] = pc_ref[...].astype(F32)
    p_scr[nq + 1] = pn_ref[0].astype(F32)

    kj = lax.broadcasted_iota(jnp.int32, (wb, 3 * wb), 1)
    qi = lax.broadcasted_iota(jnp.int32, (wb, 3 * wb), 0)
    band = jnp.abs(kj - wb - qi) <= WINDOW
    lane_kv = lax.broadcasted_iota(jnp.int32, (3 * wb, LANES), 1)
    low_half = lane_kv < GQA_HEAD_DIM
    ones_even = (lane_kv == GQA_HEAD_DIM).astype(BF16)
    ones_odd = (lane_kv == 0).astype(BF16)
    low_out = lax.broadcasted_iota(jnp.int32, (wb, LANES), 1) < GQA_HEAD_DIM

    def score_phase(a):
        rows = slice(a * wb, (a + 3) * wb)
        pc = p_scr[a + 1]
        pq = jnp.broadcast_to(pc, (wb, wb)).T
        dist = jnp.concatenate(
            [jnp.abs(pq - p_scr[a]), jnp.abs(pq - pc), jnp.abs(pq - p_scr[a + 2])], axis=1)
        key_idx = (j * nq + a) * wb - wb + kj
        mask = band & (key_idx >= 0) & (key_idx < seq)
        nd = jnp.where(mask, -dist, -jnp.inf)
        k_half = []
        for hk in range(GQA_KV_HEADS):
            kd = k_scr[rows, hk * LANES:(hk + 1) * LANES]
            zero = jnp.zeros_like(kd)
            k_half.append((jnp.where(low_half, kd, zero), jnp.where(low_half, zero, kd)))
        scores = []
        for hq in range(GQA_Q_HEADS):
            pair, par, hk = hq // 2, hq % 2, hq // GQA_GROUP
            qp = q_ref[0, a * wb:(a + 1) * wb, pair * LANES:(pair + 1) * LANES]
            scores.append(_dot_nt(qp, k_half[hk][par]) + (slopes[hq] * LOG2E) * nd)
        return scores

    def value_phase(a, scores):
        rows = slice(a * wb, (a + 3) * wb)
        v_half = []
        for hk in range(GQA_KV_HEADS):
            vd = v_scr[rows, hk * LANES:(hk + 1) * LANES]
            v_half.append((jnp.where(low_half, vd, ones_even), jnp.where(low_half, ones_odd, vd)))
        probs, sink_terms = [], []
        for hq, s in enumerate(scores):
            sink = sink_ref[hq] * LOG2E
            m = jnp.maximum(jnp.max(s, axis=-1, keepdims=True), sink)
            probs.append(jnp.exp2(s - m).astype(BF16))
            sink_terms.append(jnp.exp2(sink - m))
        outs = []
        for pair in range(GQA_Q_HEADS // 2):
            hk = (2 * pair) // GQA_GROUP
            even, odd = 2 * pair, 2 * pair + 1
            acc_e = _dot(probs[even], v_half[hk][0])
            acc_o = _dot(probs[odd], v_half[hk][1])
            r_e = 1.0 / (acc_e[:, GQA_HEAD_DIM:GQA_HEAD_DIM + 1] + sink_terms[even])
            r_o = 1.0 / (acc_o[:, 0:1] + sink_terms[odd])
            outs.append(jnp.where(low_out, acc_e * r_e, acc_o * r_o))
        o_ref[0, a * wb:(a + 1) * wb, :] = jnp.concatenate(outs, axis=-1).astype(BF16)

    scores = score_phase(0)
    for a in range(nq):
        scores_next = score_phase(a + 1) if a + 1 < nq else None
        value_phase(a, scores)
        scores = scores_next


def _gqa(qkv, pos_blocks, sink, slopes):
    b, s, _ = qkv.shape
    wb = WIN_BLOCK
    nb = s // wb
    nq = GQA_NQ
    r = nq * wb
    prev = lambda j: jnp.maximum(j * nq - 1, 0)
    nxt = lambda j: jnp.minimum((j + 1) * nq, nb - 1)
    edge = lambda lane_blk, f: pl.BlockSpec((1, wb, 2 * LANES), lambda bi, j: (bi, f(j), lane_blk))
    own = lambda lane_blk: pl.BlockSpec((1, r, 2 * LANES), lambda bi, j: (bi, j, lane_blk))
    pedge = lambda f: pl.BlockSpec((1, 1, wb), lambda bi, j: (bi * nb + f(j), 0, 0))
    qw = GQA_Q_HEADS * GQA_HEAD_DIM
    return pl.pallas_call(
        functools.partial(_gqa_kernel, seq=s, nq=nq, slopes=slopes),
        grid=(b, nb // nq),
        in_specs=[pl.BlockSpec(memory_space=pltpu.SMEM),
                  pl.BlockSpec((1, r, qw), lambda bi, j: (bi, j, 0)),
                  edge(2, prev), own(2), edge(2, nxt),
                  edge(3, prev), own(3), edge(3, nxt),
                  pedge(prev),
                  pl.BlockSpec((nq, 1, wb), lambda bi, j: (bi * (nb // nq) + j, 0, 0)),
                  pedge(nxt)],
        out_specs=pl.BlockSpec((1, r, qw), lambda bi, j: (bi, j, 0)),
        out_shape=jax.ShapeDtypeStruct((b, s, qw), BF16),
        scratch_shapes=[pltpu.VMEM((r + 2 * wb, 2 * LANES), BF16),
                        pltpu.VMEM((r + 2 * wb, 2 * LANES), BF16),
                        pltpu.VMEM((nq + 2, 1, wb), F32)],
        compiler_params=_params(2),
        name="gqa_window",
    )(sink, qkv, qkv, qkv, qkv, qkv, qkv, qkv, pos_blocks, pos_blocks, pos_blocks)


def _token_kernel(x_ref, oa_ref, ob_ref, oc_ref, gate_ref, p_ref,
                  wa_ref, wb_ref, wc_ref, wo_ref, g_mix_ref,
                  g_pre_ref, wg_ref, wu_ref, wd_ref, g_post_ref,
                  wp_ref, wpg_ref, g_ple_ref, out_ref):
    d = D_MODEL
    merged = (gate_ref[:, 0:d].astype(F32) * _dot(oa_ref[...], wa_ref[...])
              + gate_ref[:, d:2 * d].astype(F32) * _dot(ob_ref[...], wb_ref[...])
              + gate_ref[:, 2 * d:3 * d].astype(F32) * _dot(oc_ref[...], wc_ref[...]))
    x = x_ref[...] + _rms(_dot(merged.astype(BF16), wo_ref[...]), g_mix_ref[...])

    h = _rms(x, g_pre_ref[...]).astype(BF16)
    ff = jnp.zeros(x.shape, F32)
    for c in range(FFN_DIM // FFN_CHUNK):
        sl = slice(c * FFN_CHUNK, (c + 1) * FFN_CHUNK)
        a = jax.nn.silu(_dot(h, wg_ref[:, sl])) * _dot(h, wu_ref[:, sl])
        ff = ff + _dot(a.astype(BF16), wd_ref[sl, :])
    x = x + _rms(ff, g_post_ref[...])

    e = _dot(p_ref[...].astype(BF16), wp_ref[...]) * jax.nn.sigmoid(
        _dot(x.astype(BF16), wpg_ref[...]))
    out_ref[...] = x + _rms(e, g_ple_ref[...])


def _token_tail(x2, oa, ob, oc, gates, p2, wa, wb, wc, wo, g_mix,
                g_pre, wg, wu, wd, g_post, wp, wpg, g_ple):
    t = x2.shape[0]
    tm = TOKEN_TILE
    d = D_MODEL
    row = lambda n: pl.BlockSpec((tm, n), lambda i: (i, 0))
    vec = _const_spec((1, d))
    return pl.pallas_call(
        _token_kernel,
        grid=(t // tm,),
        in_specs=[row(d), row(oa.shape[1]), row(d), row(oc.shape[1]), row(3 * d), row(PLE_DIM),
                  _const_spec(wa.shape), _const_spec(wb.shape), _const_spec(wc.shape),
                  _const_spec(wo.shape), vec,
                  vec, _const_spec(wg.shape), _const_spec(wu.shape), _const_spec(wd.shape), vec,
                  _const_spec(wp.shape), _const_spec(wpg.shape), vec],
        out_specs=row(d),
        out_shape=jax.ShapeDtypeStruct((t, d), F32),
        compiler_params=_params(1),
        name="token_tail",
    )(x2, oa, ob, oc, gates, p2, wa, wb, wc, wo, g_mix, g_pre, wg, wu, wd, g_post, wp, wpg, g_ple)


def _cos_sin(num, den):
    ang = (num % den).astype(F32) * (2.0 * math.pi / den)
    return jnp.cos(ang), jnp.sin(ang)


def _dft_tables(seq):
    gd = FNET_GROUP_DIM
    jk = jnp.arange(gd, dtype=jnp.int32)
    c, s = _cos_sin(jk[:, None] * jk[None, :], gd)
    chan = (jnp.concatenate([c, -s], axis=1) * gd ** -0.5).astype(BF16)

    n1, n2 = FFT_N1, seq // FFT_N1
    a = jnp.arange(n1, dtype=jnp.int32)
    c1, s1 = _cos_sin(a[:, None] * a[None, :], n1)
    m1 = (jnp.concatenate([jnp.concatenate([c1, s1], axis=1),
                           jnp.concatenate([-s1, c1], axis=1)], axis=0) * n1 ** -0.5).astype(BF16)

    k1 = jnp.arange(n1, dtype=jnp.int32)[:, None]
    k2 = jnp.arange(n2, dtype=jnp.int32)[:, None]
    nn = jnp.arange(n2, dtype=jnp.int32)[None, :]
    ca, sa = _cos_sin(k1 * nn, seq)
    cb, sb = _cos_sin(k2 * nn, n2)
    cd = ca[:, None, :] * cb[None] - sa[:, None, :] * sb[None]
    sd = sa[:, None, :] * cb[None] + ca[:, None, :] * sb[None]
    dmat = (jnp.concatenate([cd, sd], axis=2) * n2 ** -0.5).astype(BF16)
    return chan, m1, dmat


def _head_blocks(parts):
    rows, heads = parts[0].shape[:2]
    used = sum(p.shape[2] for p in parts)
    pad = jnp.zeros((rows, heads, HEAD_PAD - used), parts[0].dtype)
    return jnp.concatenate(list(parts) + [pad], axis=2).reshape(rows, heads * HEAD_PAD)


def _prep_layer(i, w_in, w_uq, w_ukv):
    wi = w_in[i]
    d = D_MODEL
    qc = (wi[:, 544:1056] * ((GQA_HEAD_DIM ** -0.5) * LOG2E)).astype(BF16)
    wi = wi.astype(BF16)
    dup = lambda w: jnp.concatenate(
        [w[:, :GQA_HEAD_DIM], w[:, :GQA_HEAD_DIM], w[:, GQA_HEAD_DIM:], w[:, GQA_HEAD_DIM:]], axis=1)
    zcols = lambda n: jnp.zeros((d, n), BF16)
    w_in_p = jnp.concatenate(
        [wi[:, :512], zcols(MLA_NOPE), wi[:, 512:544], zcols(HEAD_PAD - MLA_QK), qc,
         dup(wi[:, 1056:1184]), dup(wi[:, 1184:1312]), wi[:, 1312:]], axis=1)

    uq = w_uq[i].reshape(MLA_Q_LORA, MLA_HEADS, MLA_QK) * ((MLA_QK ** -0.5) * LOG2E)
    wq = _head_blocks([uq]).astype(BF16)

    ukv = w_ukv[i].reshape(MLA_KV_LORA, MLA_HEADS, MLA_NOPE + MLA_V)
    wk = _head_blocks([ukv[..., :MLA_NOPE]]).astype(BF16)
    wv = _head_blocks([ukv[..., MLA_NOPE:]]).astype(BF16)
    return w_in_p, wq, wk, wv


def kernel(x, p, positions, norm_mix_pre, w_in, mla_q_norm, w_uq, mla_kv_norm, w_ukv, gqa_sink,
           w_branch_a, w_branch_b, w_branch_c, w_out, norm_mix_post, norm_ffn_pre, w_ffn_gate,
           w_ffn_up, w_ffn_down, norm_ffn_post, w_ple_proj, w_ple_gate, norm_ple):
    b, s, d = x.shape
    depth = w_in.shape[0]
    t = b * s
    n1, n2 = FFT_N1, s // FFT_N1
    assert d == D_MODEL and t % MLA_PROJ_TILE == 0 and s % FLASH_TQ == 0 and s % FLASH_TK == 0
    assert n2 % FFT_A_N2_BLOCK == 0 and d % FFT_A_COLS == 0 and (s // WIN_BLOCK) % GQA_NQ == 0

    chan, m1, dmat = _dft_tables(s)
    half = MLA_ROPE // 2
    inv_freq = ROPE_THETA ** (-jnp.arange(half, dtype=F32) / half)
    invf_lane = jnp.zeros((1, LANES), F32)
    invf_lane = invf_lane.at[0, MLA_NOPE:MLA_NOPE + half].set(inv_freq)
    invf_lane = invf_lane.at[0, MLA_NOPE + half:MLA_NOPE + MLA_ROPE].set(inv_freq)
    posf = positions.reshape(t, 1).astype(F32)
    pos_blocks = positions.reshape(b * (s // WIN_BLOCK), 1, WIN_BLOCK)
    vbias = jnp.zeros((MLA_HEADS, HEAD_PAD), F32).at[:, MLA_V].set(1.0).reshape(1, -1)
    slopes = tuple(float(v) for v in
                   2.0 ** (-8.0 * (np.arange(GQA_Q_HEADS, dtype=np.float32) + 1.0) / GQA_Q_HEADS))
    r1 = lambda v: v.reshape(1, -1)
    (wa_all, wb_all, wc_all, wo_all, wg_all, wu_all, wd_all, wp_all, wpg_all) = [
        w.astype(BF16) for w in (w_branch_a, w_branch_b, w_branch_c, w_out, w_ffn_gate, w_ffn_up,
                                 w_ffn_down, w_ple_proj, w_ple_gate)]

    rope_cos, rope_sin = _rope_table(posf, invf_lane)

    x2 = x.reshape(t, d)
    for i in range(depth):
        w_in_p, wq, wk, wv = _prep_layer(i, w_in, w_uq, w_ukv)
        cq, ckv, qkv, gates, wr, wi = _inproj(x2, r1(norm_mix_pre[i]), w_in_p, chan)

        q, k, v = _mla_proj(cq, ckv, rope_cos, rope_sin, r1(mla_q_norm[i]), r1(mla_kv_norm[i]),
                            wq, wk, wv, vbias)
        hw = MLA_HEADS * HEAD_PAD
        o_a = _flash(q.reshape(b, s, hw), k.reshape(b, s, hw), v.reshape(b, s, hw))

        ar, ai = _fft_a(wr.reshape(b, n1, n2, d), wi.reshape(b, n1, n2, d), m1)
        o_b = _fft_b(ar, ai, dmat)

        o_c = _gqa(qkv.reshape(b, s, -1), pos_blocks, gqa_sink[i], slopes)

        x2 = _token_tail(
            x2, o_a.reshape(t, -1), o_b.reshape(t, d), o_c.reshape(t, -1), gates,
            p[i].reshape(t, PLE_DIM),
            wa_all[i], wb_all[i], wc_all[i], wo_all[i],
            r1(norm_mix_post[i]), r1(norm_ffn_pre[i]), wg_all[i], wu_all[i],
            wd_all[i], r1(norm_ffn_post[i]), wp_all[i], wpg_all[i],
            r1(norm_ple[i]))
    return x2.reshape(b, s, d)
```

```python
import functools
import math

import numpy as np
import jax
import jax.numpy as jnp
from jax import lax
from jax.experimental import pallas as pl
from jax.experimental.pallas import tpu as pltpu

F32 = jnp.float32
BF16 = jnp.bfloat16

D_MODEL = 1024
PLE_DIM = 256
MLA_HEADS = 8
MLA_NOPE = 64
MLA_ROPE = 32
MLA_V = 64
MLA_QK = MLA_NOPE + MLA_ROPE
MLA_Q_LORA = 384
MLA_KV_LORA = 128
ROPE_THETA = 10000.0
FNET_GROUPS = 4
FNET_GROUP_DIM = D_MODEL // FNET_GROUPS
GQA_Q_HEADS = 8
GQA_KV_HEADS = 2
GQA_GROUP = GQA_Q_HEADS // GQA_KV_HEADS
GQA_HEAD_DIM = 64
WINDOW = 128
WIN_BLOCK = 128
FFN_DIM = 2816
RMS_EPS = 1e-6

LANES = 128
HEAD_PAD = 128
FFT_N1 = 128
VMEM_LIMIT = 56 * 1024 * 1024

TOKEN_TILE = 512
MLA_PROJ_TILE = 1024
FLASH_TQ = 1024
FLASH_TK = 1024
GQA_NQ = 8
FFT_A_N2_BLOCK = 16
FFT_A_COLS = 512
FFT_B_ROWS = 16
FFN_CHUNK = 256
LOG2E = 1.4426950408889634

SEG_CQ = (0, 384)
SEG_CKV = (384, 640)
SEG_QKV = (640, 1664)
SEG_GATE = (1664, 4736)
IN_COLS_PAD = 4736


def _params(n_parallel_axes):
    return pltpu.CompilerParams(
        dimension_semantics=("arbitrary",) * n_parallel_axes,
        vmem_limit_bytes=VMEM_LIMIT)


def _const_spec(shape):
    zeros = (0,) * len(shape)
    return pl.BlockSpec(shape, lambda *_: zeros, pipeline_mode=pl.Buffered(1))


def _layer_spec(shape, layer):
    zeros = (0,) * len(shape)
    return pl.BlockSpec((None,) + tuple(shape), lambda *_: (layer,) + zeros,
                        pipeline_mode=pl.Buffered(1))


def _rms(x, gain):
    ms = jnp.mean(x * x, axis=-1, keepdims=True)
    return x * lax.rsqrt(ms + RMS_EPS) * gain


def _dot(a, b):
    return jnp.dot(a, b, preferred_element_type=F32)


def _dot_nt(a, b):
    return lax.dot_general(a, b, (((1,), (1,)), ((), ())), preferred_element_type=F32)


def _inproj_kernel(x_ref, gain_ref, w_ref, dft_ref,
                   cq_ref, ckv_ref, qkv_ref, gate_ref, wr_ref, wi_ref):
    h = _rms(x_ref[...], gain_ref[...]).astype(BF16)
    cq_ref[...] = _dot(h, w_ref[:, SEG_CQ[0]:SEG_CQ[1]]).astype(BF16)
    ckv_ref[...] = _dot(h, w_ref[:, SEG_CKV[0]:SEG_CKV[1]]).astype(BF16)
    qkv_ref[...] = _dot(h, w_ref[:, SEG_QKV[0]:SEG_QKV[1]]).astype(BF16)
    for c in range(3):
        lo = SEG_GATE[0] + c * D_MODEL
        g = _dot(h, w_ref[:, lo:lo + D_MODEL])
        gate_ref[:, c * D_MODEL:(c + 1) * D_MODEL] = jax.nn.sigmoid(g).astype(BF16)
    gd = FNET_GROUP_DIM
    for g in range(FNET_GROUPS):
        yz = _dot(h[:, g * gd:(g + 1) * gd], dft_ref[...])
        wr_ref[:, g * gd:(g + 1) * gd] = yz[:, :gd].astype(BF16)
        wi_ref[:, g * gd:(g + 1) * gd] = yz[:, gd:].astype(BF16)


def _inproj(x2, gain, w_in_p, dft_c, layer):
    t = x2.shape[0]
    tm = TOKEN_TILE
    row = lambda n: pl.BlockSpec((tm, n), lambda i: (i, 0))
    outs = [(t, 384), (t, 256), (t, SEG_QKV[1] - SEG_QKV[0]), (t, 3 * D_MODEL), (t, D_MODEL),
            (t, D_MODEL)]
    return pl.pallas_call(
        _inproj_kernel,
        grid=(t // tm,),
        in_specs=[row(D_MODEL), _const_spec((1, D_MODEL)),
                  _layer_spec((D_MODEL, IN_COLS_PAD), layer),
                  _const_spec((FNET_GROUP_DIM, 2 * FNET_GROUP_DIM))],
        out_specs=[row(n) for _, n in outs],
        out_shape=[jax.ShapeDtypeStruct(s, BF16) for s in outs],
        compiler_params=_params(1),
        name="inproj",
    )(x2, gain, w_in_p, dft_c)


def _rope_table_kernel(pos_ref, invf_ref, cos_ref, sin_ref):
    ang = pos_ref[...] * invf_ref[...]
    cos_ref[...] = jnp.cos(ang)
    sin_ref[...] = jnp.sin(ang)


def _rope_table(posf, invf_lane):
    t = posf.shape[0]
    tm = TOKEN_TILE
    row = lambda n: pl.BlockSpec((tm, n), lambda i: (i, 0))
    return pl.pallas_call(
        _rope_table_kernel,
        grid=(t // tm,),
        in_specs=[row(1), _const_spec((1, LANES))],
        out_specs=[row(LANES)] * 2,
        out_shape=[jax.ShapeDtypeStruct((t, LANES), F32)] * 2,
        compiler_params=_params(1),
        name="rope_table",
    )(posf, invf_lane)


def _mla_proj_kernel(cq_ref, ckv_ref, cos_ref, sin_ref, qn_ref, kvn_ref,
                     wq_ref, wk_ref, wv_ref, vb_ref, q_ref, k_ref, v_ref):
    cqn = _rms(cq_ref[...].astype(F32), qn_ref[...]).astype(BF16)
    ckvn = _rms(ckv_ref[:, :MLA_KV_LORA].astype(F32), kvn_ref[...]).astype(BF16)
    cos = cos_ref[...]
    sin = sin_ref[...]
    half = MLA_ROPE // 2
    lane = lax.broadcasted_iota(jnp.int32, cos.shape, 1)
    second = lane >= MLA_NOPE + half
    sin_up = jnp.where(second, sin, 0.0)
    sin_dn = jnp.where(second, 0.0, -sin)

    def rope(x):
        return (x * cos + pltpu.roll(x, half, 1) * sin_up
                + pltpu.roll(x, HEAD_PAD - half, 1) * sin_dn)

    kr = rope(ckv_ref[:, MLA_KV_LORA:].astype(F32))
    for h in range(MLA_HEADS):
        sl = slice(h * HEAD_PAD, (h + 1) * HEAD_PAD)
        q_ref[:, sl] = rope(_dot(cqn, wq_ref[:, sl])).astype(BF16)
        k_ref[:, sl] = (_dot(ckvn, wk_ref[:, sl]) + kr).astype(BF16)
        v_ref[:, sl] = (_dot(ckvn, wv_ref[:, sl]) + vb_ref[:, sl]).astype(BF16)


def _mla_proj(cq, ckv, cos, sin, qn, kvn, wq, wk, wv, vb, layer):
    t = cq.shape[0]
    tm = MLA_PROJ_TILE
    hw = MLA_HEADS * HEAD_PAD
    row = lambda n: pl.BlockSpec((tm, n), lambda i: (i, 0))
    return pl.pallas_call(
        _mla_proj_kernel,
        grid=(t // tm,),
        in_specs=[row(MLA_Q_LORA), row(256), row(LANES), row(LANES),
                  _const_spec((1, MLA_Q_LORA)), _const_spec((1, MLA_KV_LORA)),
                  _layer_spec((MLA_Q_LORA, hw), layer), _layer_spec((MLA_KV_LORA, hw), layer),
                  _layer_spec((MLA_KV_LORA, hw), layer), _const_spec((1, hw))],
        out_specs=[row(hw)] * 3,
        out_shape=[jax.ShapeDtypeStruct((t, hw), BF16)] * 3,
        compiler_params=_params(1),
        name="mla_proj",
    )(cq, ckv, cos, sin, qn, kvn, wq, wk, wv, vb)


def _flash_kernel(q_ref, k_ref, v_ref, o_ref, *, tk, nk):
    tq = q_ref.shape[1]
    heads = [slice(h * HEAD_PAD, (h + 1) * HEAD_PAD) for h in range(2)]
    qs = [q_ref[0, :, sl] for sl in heads]

    def scores(i):
        return [_dot_nt(q, k_ref[0, i * tk:(i + 1) * tk, sl]) for q, sl in zip(qs, heads)]

    state = [(jnp.full((tq, 1), -jnp.inf, F32), jnp.zeros((tq, HEAD_PAD), F32))] * 2
    ss = scores(0)
    for i in range(nk):
        ss_next = scores(i + 1) if i + 1 < nk else None
        new = []
        for (m, acc), s, sl in zip(state, ss, heads):
            m_new = jnp.maximum(m, jnp.max(s, axis=-1, keepdims=True))
            p = jnp.exp2((s - m_new).astype(BF16))
            alpha = jnp.exp2(m - m_new)
            v = v_ref[0, i * tk:(i + 1) * tk, sl]
            new.append((m_new, alpha * acc + _dot(p, v)))
        state, ss = new, ss_next
    outs = [acc[:, :MLA_V] / acc[:, MLA_V:MLA_V + 1] for _, acc in state]
    o_ref[0] = jnp.concatenate(outs, axis=-1).astype(BF16)


def _flash(q, k, v):
    b, s, hw = q.shape
    tq, tk = FLASH_TQ, FLASH_TK
    pair = 2 * HEAD_PAD
    return pl.pallas_call(
        functools.partial(_flash_kernel, tk=tk, nk=s // tk),
        grid=(b, MLA_HEADS // 2, s // tq),
        in_specs=[pl.BlockSpec((1, tq, pair), lambda bi, hi, qi: (bi, qi, hi)),
                  pl.BlockSpec((1, s, pair), lambda bi, hi, qi: (bi, 0, hi)),
                  pl.BlockSpec((1, s, pair), lambda bi, hi, qi: (bi, 0, hi))],
        out_specs=pl.BlockSpec((1, tq, 2 * MLA_V), lambda bi, hi, qi: (bi, qi, hi)),
        out_shape=jax.ShapeDtypeStruct((b, s, MLA_HEADS * MLA_V), BF16),
        compiler_params=_params(3),
        name="mla_flash",
    )(q, k, v)


def _fft_a_kernel(xr_ref, xi_ref, m1_ref, ar_ref, ai_ref):
    xr = jnp.swapaxes(xr_ref[0], 0, 1)
    xi = jnp.swapaxes(xi_ref[0], 0, 1)
    out_r, out_i = [], []
    for j in range(xr.shape[0]):
        a = _dot(m1_ref[...], jnp.concatenate([xr[j], xi[j]], axis=0))
        out_r.append(a[:FFT_N1].astype(BF16))
        out_i.append(a[FFT_N1:].astype(BF16))
    ar_ref[0] = jnp.stack(out_r)
    ai_ref[0] = jnp.stack(out_i)


def _fft_a(wr, wi, m1):
    b, n1, n2, c = wr.shape
    nb2, tc = FFT_A_N2_BLOCK, FFT_A_COLS
    blk = pl.BlockSpec((1, n1, nb2, tc), lambda bi, ji, ci: (bi, 0, ji, ci))
    out_blk = pl.BlockSpec((1, nb2, n1, tc), lambda bi, ji, ci: (bi, ji, 0, ci))
    return pl.pallas_call(
        _fft_a_kernel,
        grid=(b, n2 // nb2, c // tc),
        in_specs=[blk, blk, _const_spec((2 * n1, 2 * n1))],
        out_specs=[out_blk, out_blk],
        out_shape=[jax.ShapeDtypeStruct((b, n2, n1, c), BF16)] * 2,
        compiler_params=_params(3),
        name="fft_a",
    )(wr, wi, m1)


def _fft_b_kernel(ar_ref, ai_ref, d_ref, o_ref, *, kb):
    ar = jnp.swapaxes(ar_ref[0], 0, 1)
    ai = jnp.swapaxes(ai_ref[0], 0, 1)
    res = []
    for j in range(kb):
        a = jnp.concatenate([ar[j], ai[j]], axis=0)
        res.append(_dot(d_ref[j], a).astype(BF16))
    o_ref[0] = jnp.swapaxes(jnp.stack(res), 0, 1)


def _fft_b(ar, ai, dmat):
    b, n2, n1, c = ar.shape
    kb = FFT_B_ROWS
    blk = pl.BlockSpec((1, n2, kb, c), lambda bi, ki: (bi, 0, ki, 0))
    return pl.pallas_call(
        functools.partial(_fft_b_kernel, kb=kb),
        grid=(b, n1 // kb),
        in_specs=[blk, blk, pl.BlockSpec((kb, n2, 2 * n2), lambda bi, ki: (ki, 0, 0))],
        out_specs=pl.BlockSpec((1, n2, kb, c), lambda bi, ki: (bi, 0, ki, 0)),
        out_shape=jax.ShapeDtypeStruct((b, n2, n1, c), BF16),
        compiler_params=_params(2),
        name="fft_b",
    )(ar, ai, dmat)


def _gqa_kernel(sink_ref, q_ref, kp_ref, kc_ref, kn_ref, vp_ref, vc_ref, vn_ref,
                pp_ref, pc_ref, pn_ref, o_ref, k_scr, v_scr, p_scr, *, seq, nq, slopes):
    j = pl.program_id(1)
    wb = WIN_BLOCK
    r = nq * wb
    k_scr[0:wb] = kp_ref[0]
    k_scr[wb:wb + r] = kc_ref[0]
    k_scr[wb + r:] = kn_ref[0]
    v_scr[0:wb] = vp_ref[0]
    v_scr[wb:wb + r] = vc_ref[0]
    v_scr[wb + r:] = vn_ref[0]
    p_scr[0] = pp_ref[0].astype(F32)
    p_scr[1:nq + 1] = pc_ref[...].astype(F32)
    p_scr[nq + 1] = pn_ref[0].astype(F32)

    kj = lax.broadcasted_iota(jnp.int32, (wb, 3 * wb), 1)
    qi = lax.broadcasted_iota(jnp.int32, (wb, 3 * wb), 0)
    band = jnp.abs(kj - wb - qi) <= WINDOW
    lane_kv = lax.broadcasted_iota(jnp.int32, (3 * wb, LANES), 1)
    low_half = lane_kv < GQA_HEAD_DIM
    ones_even = (lane_kv == GQA_HEAD_DIM).astype(BF16)
    ones_odd = (lane_kv == 0).astype(BF16)
    low_out = lax.broadcasted_iota(jnp.int32, (wb, LANES), 1) < GQA_HEAD_DIM

    def score_phase(a):
        rows = slice(a * wb, (a + 3) * wb)
        pc = p_scr[a + 1]
        pq = jnp.broadcast_to(pc, (wb, wb)).T
        dist = jnp.concatenate(
            [jnp.abs(pq - p_scr[a]), jnp.abs(pq - pc), jnp.abs(pq - p_scr[a + 2])], axis=1)
        key_idx = (j * nq + a) * wb - wb + kj
        mask = band & (key_idx >= 0) & (key_idx < seq)
        nd = jnp.where(mask, -dist, -jnp.inf)
        k_half = []
        for hk in range(GQA_KV_HEADS):
            kd = k_scr[rows, hk * LANES:(hk + 1) * LANES]
            zero = jnp.zeros_like(kd)
            k_half.append((jnp.where(low_half, kd, zero), jnp.where(low_half, zero, kd)))
        scores = []
        for hq in range(GQA_Q_HEADS):
            pair, par, hk = hq // 2, hq % 2, hq // GQA_GROUP
            qp = q_ref[0, a * wb:(a + 1) * wb, pair * LANES:(pair + 1) * LANES]
            scores.append(_dot_nt(qp, k_half[hk][par]) + (slopes[hq] * LOG2E) * nd)
        return scores

    def value_phase(a, scores):
        rows = slice(a * wb, (a + 3) * wb)
        v_half = []
        for hk in range(GQA_KV_HEADS):
            vd = v_scr[rows, hk * LANES:(hk + 1) * LANES]
            v_half.append((jnp.where(low_half, vd, ones_even), jnp.where(low_half, ones_odd, vd)))
        probs, sink_terms = [], []
        for hq, s in enumerate(scores):
            sink = sink_ref[hq] * LOG2E
            m = jnp.maximum(jnp.max(s, axis=-1, keepdims=True), sink)
            probs.append(jnp.exp2(s - m).astype(BF16))
            sink_terms.append(jnp.exp2(sink - m))
        outs = []
        for pair in range(GQA_Q_HEADS // 2):
            hk = (2 * pair) // GQA_GROUP
            even, odd = 2 * pair, 2 * pair + 1
            acc_e = _dot(probs[even], v_half[hk][0])
            acc_o = _dot(probs[odd], v_half[hk][1])
            r_e = 1.0 / (acc_e[:, GQA_HEAD_DIM:GQA_HEAD_DIM + 1] + sink_terms[even])
            r_o = 1.0 / (acc_o[:, 0:1] + sink_terms[odd])
            outs.append(jnp.where(low_out, acc_e * r_e, acc_o * r_o))
        o_ref[0, a * wb:(a + 1) * wb, :] = jnp.concatenate(outs, axis=-1).astype(BF16)

    scores = score_phase(0)
    for a in range(nq):
        scores_next = score_phase(a + 1) if a + 1 < nq else None
        value_phase(a, scores)
        scores = scores_next


def _gqa(qkv, pos_blocks, sink, slopes):
    b, s, _ = qkv.shape
    wb = WIN_BLOCK
    nb = s // wb
    nq = GQA_NQ
    r = nq * wb
    prev = lambda j: jnp.maximum(j * nq - 1, 0)
    nxt = lambda j: jnp.minimum((j + 1) * nq, nb - 1)
    edge = lambda lane_blk, f: pl.BlockSpec((1, wb, 2 * LANES), lambda bi, j: (bi, f(j), lane_blk))
    own = lambda lane_blk: pl.BlockSpec((1, r, 2 * LANES), lambda bi, j: (bi, j, lane_blk))
    pedge = lambda f: pl.BlockSpec((1, 1, wb), lambda bi, j: (bi * nb + f(j), 0, 0))
    qw = GQA_Q_HEADS * GQA_HEAD_DIM
    return pl.pallas_call(
        functools.partial(_gqa_kernel, seq=s, nq=nq, slopes=slopes),
        grid=(b, nb // nq),
        in_specs=[pl.BlockSpec(memory_space=pltpu.SMEM),
                  pl.BlockSpec((1, r, qw), lambda bi, j: (bi, j, 0)),
                  edge(2, prev), own(2), edge(2, nxt),
                  edge(3, prev), own(3), edge(3, nxt),
                  pedge(prev),
                  pl.BlockSpec((nq, 1, wb), lambda bi, j: (bi * (nb // nq) + j, 0, 0)),
                  pedge(nxt)],
        out_specs=pl.BlockSpec((1, r, qw), lambda bi, j: (bi, j, 0)),
        out_shape=jax.ShapeDtypeStruct((b, s, qw), BF16),
        scratch_shapes=[pltpu.VMEM((r + 2 * wb, 2 * LANES), BF16),
                        pltpu.VMEM((r + 2 * wb, 2 * LANES), BF16),
                        pltpu.VMEM((nq + 2, 1, wb), F32)],
        compiler_params=_params(2),
        name="gqa_window",
    )(sink, qkv, qkv, qkv, qkv, qkv, qkv, qkv, pos_blocks, pos_blocks, pos_blocks)


def _token_kernel(x_ref, oa_ref, ob_ref, oc_ref, gate_ref, p_ref,
                  wa_ref, wb_ref, wc_ref, wo_ref, g_mix_ref,
                  g_pre_ref, wg_ref, wu_ref, wd_ref, g_post_ref,
                  wp_ref, wpg_ref, g_ple_ref, out_ref):
    d = D_MODEL
    merged = (gate_ref[:, 0:d].astype(F32) * _dot(oa_ref[...], wa_ref[...])
              + gate_ref[:, d:2 * d].astype(F32) * _dot(ob_ref[...], wb_ref[...])
              + gate_ref[:, 2 * d:3 * d].astype(F32) * _dot(oc_ref[...], wc_ref[...]))
    x = x_ref[...] + _rms(_dot(merged.astype(BF16), wo_ref[...]), g_mix_ref[...])

    h = _rms(x, g_pre_ref[...]).astype(BF16)
    ff = jnp.zeros(x.shape, F32)
    for c in range(FFN_DIM // FFN_CHUNK):
        sl = slice(c * FFN_CHUNK, (c + 1) * FFN_CHUNK)
        a = jax.nn.silu(_dot(h, wg_ref[:, sl])) * _dot(h, wu_ref[:, sl])
        ff = ff + _dot(a.astype(BF16), wd_ref[sl, :])
    x = x + _rms(ff, g_post_ref[...])

    e = _dot(p_ref[...].astype(BF16), wp_ref[...]) * jax.nn.sigmoid(
        _dot(x.astype(BF16), wpg_ref[...]))
    out_ref[...] = x + _rms(e, g_ple_ref[...])


def _token_tail(x2, oa, ob, oc, gates, p2, wa, wb, wc, wo, g_mix,
                g_pre, wg, wu, wd, g_post, wp, wpg, g_ple, layer):
    t = x2.shape[0]
    tm = TOKEN_TILE
    d = D_MODEL
    row = lambda n: pl.BlockSpec((tm, n), lambda i: (i, 0))
    p_row = pl.BlockSpec((tm, PLE_DIM), lambda i: (i + layer * (t // tm), 0))
    vec = _const_spec((1, d))
    w = lambda a: _layer_spec(a.shape[1:], layer)
    return pl.pallas_call(
        _token_kernel,
        grid=(t // tm,),
        in_specs=[row(d), row(oa.shape[1]), row(d), row(oc.shape[1]), row(3 * d), p_row,
                  w(wa), w(wb), w(wc), w(wo), vec,
                  vec, w(wg), w(wu), w(wd), vec,
                  w(wp), w(wpg), vec],
        out_specs=row(d),
        out_shape=jax.ShapeDtypeStruct((t, d), F32),
        compiler_params=_params(1),
        name="token_tail",
    )(x2, oa, ob, oc, gates, p2, wa, wb, wc, wo, g_mix, g_pre, wg, wu, wd, g_post, wp, wpg, g_ple)


def _cos_sin(num, den):
    ang = (num % den).astype(F32) * (2.0 * math.pi / den)
    return jnp.cos(ang), jnp.sin(ang)


def _dft_tables(seq):
    gd = FNET_GROUP_DIM
    jk = jnp.arange(gd, dtype=jnp.int32)
    c, s = _cos_sin(jk[:, None] * jk[None, :], gd)
    chan = (jnp.concatenate([c, -s], axis=1) * gd ** -0.5).astype(BF16)

    n1, n2 = FFT_N1, seq // FFT_N1
    a = jnp.arange(n1, dtype=jnp.int32)
    c1, s1 = _cos_sin(a[:, None] * a[None, :], n1)
    m1 = (jnp.concatenate([jnp.concatenate([c1, s1], axis=1),
                           jnp.concatenate([-s1, c1], axis=1)], axis=0) * n1 ** -0.5).astype(BF16)

    k1 = jnp.arange(n1, dtype=jnp.int32)[:, None]
    k2 = jnp.arange(n2, dtype=jnp.int32)[:, None]
    nn = jnp.arange(n2, dtype=jnp.int32)[None, :]
    ca, sa = _cos_sin(k1 * nn, seq)
    cb, sb = _cos_sin(k2 * nn, n2)
    cd = ca[:, None, :] * cb[None] - sa[:, None, :] * sb[None]
    sd = sa[:, None, :] * cb[None] + ca[:, None, :] * sb[None]
    dmat = (jnp.concatenate([cd, sd], axis=2) * n2 ** -0.5).astype(BF16)
    return chan, m1, dmat


def _head_blocks(w):
    depth, rows, heads, used = w.shape
    pad = jnp.zeros((depth, rows, heads, HEAD_PAD - used), w.dtype)
    return jnp.concatenate([w, pad], axis=3).reshape(depth, rows, heads * HEAD_PAD).astype(BF16)


def _prep_weights(w_in, w_uq, w_ukv):
    depth, d = w_in.shape[:2]
    qc = (w_in[..., 544:1056] * ((GQA_HEAD_DIM ** -0.5) * LOG2E)).astype(BF16)
    wi = w_in.astype(BF16)
    hd = GQA_HEAD_DIM
    dup = lambda w: jnp.concatenate([w[..., :hd], w[..., :hd], w[..., hd:], w[..., hd:]], axis=-1)
    zcols = lambda n: jnp.zeros((depth, d, n), BF16)
    w_in_p = jnp.concatenate(
        [wi[..., :512], zcols(MLA_NOPE), wi[..., 512:544], zcols(HEAD_PAD - MLA_QK), qc,
         dup(wi[..., 1056:1184]), dup(wi[..., 1184:1312]), wi[..., 1312:]], axis=-1)

    uq = w_uq.reshape(depth, MLA_Q_LORA, MLA_HEADS, MLA_QK) * ((MLA_QK ** -0.5) * LOG2E)
    wq = _head_blocks(uq)
    ukv = w_ukv.reshape(depth, MLA_KV_LORA, MLA_HEADS, MLA_NOPE + MLA_V)
    wk = _head_blocks(ukv[..., :MLA_NOPE])
    wv = _head_blocks(ukv[..., MLA_NOPE:])
    return w_in_p, wq, wk, wv


def kernel(x, p, positions, norm_mix_pre, w_in, mla_q_norm, w_uq, mla_kv_norm, w_ukv, gqa_sink,
           w_branch_a, w_branch_b, w_branch_c, w_out, norm_mix_post, norm_ffn_pre, w_ffn_gate,
           w_ffn_up, w_ffn_down, norm_ffn_post, w_ple_proj, w_ple_gate, norm_ple):
    b, s, d = x.shape
    depth = w_in.shape[0]
    t = b * s
    n1, n2 = FFT_N1, s // FFT_N1
    assert d == D_MODEL and t % MLA_PROJ_TILE == 0 and s % FLASH_TQ == 0 and s % FLASH_TK == 0
    assert n2 % FFT_A_N2_BLOCK == 0 and d % FFT_A_COLS == 0 and (s // WIN_BLOCK) % GQA_NQ == 0

    chan, m1, dmat = _dft_tables(s)
    half = MLA_ROPE // 2
    inv_freq = ROPE_THETA ** (-jnp.arange(half, dtype=F32) / half)
    invf_lane = jnp.zeros((1, LANES), F32)
    invf_lane = invf_lane.at[0, MLA_NOPE:MLA_NOPE + half].set(inv_freq)
    invf_lane = invf_lane.at[0, MLA_NOPE + half:MLA_NOPE + MLA_ROPE].set(inv_freq)
    posf = positions.reshape(t, 1).astype(F32)
    pos_blocks = positions.reshape(b * (s // WIN_BLOCK), 1, WIN_BLOCK)
    vbias = jnp.zeros((MLA_HEADS, HEAD_PAD), F32).at[:, MLA_V].set(1.0).reshape(1, -1)
    slopes = tuple(float(v) for v in
                   2.0 ** (-8.0 * (np.arange(GQA_Q_HEADS, dtype=np.float32) + 1.0) / GQA_Q_HEADS))
    r1 = lambda v: v.reshape(1, -1)
    tail_w = [w.astype(BF16) for w in (w_branch_a, w_branch_b, w_branch_c, w_out, w_ffn_gate,
                                       w_ffn_up, w_ffn_down, w_ple_proj, w_ple_gate)]
    wa_all, wb_all, wc_all, wo_all, wg_all, wu_all, wd_all, wp_all, wpg_all = tail_w
    w_in_p, wq, wk, wv = _prep_weights(w_in, w_uq, w_ukv)
    p2 = p.reshape(depth * t, PLE_DIM)

    rope_cos, rope_sin = _rope_table(posf, invf_lane)

    x2 = x.reshape(t, d)
    for i in range(depth):
        cq, ckv, qkv, gates, wr, wi = _inproj(x2, r1(norm_mix_pre[i]), w_in_p, chan, i)

        q, k, v = _mla_proj(cq, ckv, rope_cos, rope_sin, r1(mla_q_norm[i]), r1(mla_kv_norm[i]),
                            wq, wk, wv, vbias, i)
        hw = MLA_HEADS * HEAD_PAD
        o_a = _flash(q.reshape(b, s, hw), k.reshape(b, s, hw), v.reshape(b, s, hw))

        ar, ai = _fft_a(wr.reshape(b, n1, n2, d), wi.reshape(b, n1, n2, d), m1)
        o_b = _fft_b(ar, ai, dmat)

        o_c = _gqa(qkv.reshape(b, s, -1), pos_blocks, gqa_sink[i], slopes)

        x2 = _token_tail(
            x2, o_a.reshape(t, -1), o_b.reshape(t, d), o_c.reshape(t, -1), gates,
            p2, wa_all, wb_all, wc_all, wo_all,
            r1(norm_mix_post[i]), r1(norm_ffn_pre[i]), wg_all, wu_all,
            wd_all, r1(norm_ffn_post[i]), wp_all, wpg_all,
            r1(norm_ple[i]), i)
    return x2.reshape(b, s, d)
```

```python
import functools
import math

import numpy as np
import jax
import jax.numpy as jnp
from jax import lax
from jax.experimental import pallas as pl
from jax.experimental.pallas import tpu as pltpu

F32 = jnp.float32
BF16 = jnp.bfloat16

D_MODEL = 1024
PLE_DIM = 256
MLA_HEADS = 8
MLA_NOPE = 64
MLA_ROPE = 32
MLA_V = 64
MLA_QK = MLA_NOPE + MLA_ROPE
MLA_Q_LORA = 384
MLA_KV_LORA = 128
ROPE_THETA = 10000.0
FNET_GROUPS = 4
FNET_GROUP_DIM = D_MODEL // FNET_GROUPS
GQA_Q_HEADS = 8
GQA_KV_HEADS = 2
GQA_GROUP = GQA_Q_HEADS // GQA_KV_HEADS
GQA_HEAD_DIM = 64
WINDOW = 128
WIN_BLOCK = 128
FFN_DIM = 2816
RMS_EPS = 1e-6

LANES = 128
HEAD_PAD = 128
FFT_N1 = 128
VMEM_LIMIT = 56 * 1024 * 1024

TOKEN_TILE = 512
FLASH_TQ = 1024
FLASH_TK = 1024
GQA_NQ = 8
FFT_A_N2_BLOCK = 16
FFT_A_COLS = 512
FFT_B_ROWS = 16
FFN_CHUNK = 256
LOG2E = 1.4426950408889634

SEG_CQ = (0, 384)
SEG_CKV = (384, 640)
SEG_QKV = (640, 1664)
SEG_GATE = (1664, 4736)
IN_COLS_PAD = 4736


def _params(n_parallel_axes):
    return pltpu.CompilerParams(
        dimension_semantics=("arbitrary",) * n_parallel_axes,
        vmem_limit_bytes=VMEM_LIMIT)


def _const_spec(shape):
    zeros = (0,) * len(shape)
    return pl.BlockSpec(shape, lambda *_: zeros, pipeline_mode=pl.Buffered(1))


def _layer_spec(shape, layer):
    zeros = (0,) * len(shape)
    return pl.BlockSpec((None,) + tuple(shape), lambda *_: (layer,) + zeros,
                        pipeline_mode=pl.Buffered(1))


def _rms(x, gain):
    ms = jnp.mean(x * x, axis=-1, keepdims=True)
    return x * lax.rsqrt(ms + RMS_EPS) * gain


def _dot(a, b):
    return jnp.dot(a, b, preferred_element_type=F32)


def _dot_nt(a, b):
    return lax.dot_general(a, b, (((1,), (1,)), ((), ())), preferred_element_type=F32)


def _inproj_kernel(x_ref, gain_ref, w_ref, dft_ref, cos_ref, sin_ref, qn_ref, kvn_ref,
                   wq_ref, wk_ref, wv_ref, vb_ref,
                   qkv_ref, gate_ref, wr_ref, wi_ref, q_ref, k_ref, v_ref):
    h = _rms(x_ref[...], gain_ref[...]).astype(BF16)
    cq = _dot(h, w_ref[:, SEG_CQ[0]:SEG_CQ[1]])
    ckv = _dot(h, w_ref[:, SEG_CKV[0]:SEG_CKV[1]])
    qkv_ref[...] = _dot(h, w_ref[:, SEG_QKV[0]:SEG_QKV[1]]).astype(BF16)
    for c in range(3):
        lo = SEG_GATE[0] + c * D_MODEL
        g = _dot(h, w_ref[:, lo:lo + D_MODEL])
        gate_ref[:, c * D_MODEL:(c + 1) * D_MODEL] = jax.nn.sigmoid(g).astype(BF16)

    cqn = _rms(cq, qn_ref[...]).astype(BF16)
    ckvn = _rms(ckv[:, :MLA_KV_LORA], kvn_ref[...]).astype(BF16)
    cos = cos_ref[...]
    sin = sin_ref[...]
    half = MLA_ROPE // 2
    lane = lax.broadcasted_iota(jnp.int32, cos.shape, 1)
    second = lane >= MLA_NOPE + half
    sin_up = jnp.where(second, sin, 0.0)
    sin_dn = jnp.where(second, 0.0, -sin)

    def rope(x):
        return (x * cos + pltpu.roll(x, half, 1) * sin_up
                + pltpu.roll(x, HEAD_PAD - half, 1) * sin_dn)

    kr = rope(ckv[:, MLA_KV_LORA:])
    for hd in range(MLA_HEADS):
        sl = slice(hd * HEAD_PAD, (hd + 1) * HEAD_PAD)
        q_ref[:, sl] = rope(_dot(cqn, wq_ref[:, sl])).astype(BF16)
        k_ref[:, sl] = (_dot(ckvn, wk_ref[:, sl]) + kr).astype(BF16)
        v_ref[:, sl] = (_dot(ckvn, wv_ref[:, sl]) + vb_ref[:, sl]).astype(BF16)

    gd = FNET_GROUP_DIM
    for g in range(FNET_GROUPS):
        yz = _dot(h[:, g * gd:(g + 1) * gd], dft_ref[...])
        wr_ref[:, g * gd:(g + 1) * gd] = yz[:, :gd].astype(BF16)
        wi_ref[:, g * gd:(g + 1) * gd] = yz[:, gd:].astype(BF16)


def _inproj(x2, gain, w_in_p, dft_c, cos, sin, qn, kvn, wq, wk, wv, vb, layer):
    t = x2.shape[0]
    tm = TOKEN_TILE
    hw = MLA_HEADS * HEAD_PAD
    row = lambda n: pl.BlockSpec((tm, n), lambda i: (i, 0))
    outs = [(t, SEG_QKV[1] - SEG_QKV[0]), (t, 3 * D_MODEL), (t, D_MODEL), (t, D_MODEL),
            (t, hw), (t, hw), (t, hw)]
    return pl.pallas_call(
        _inproj_kernel,
        grid=(t // tm,),
        in_specs=[row(D_MODEL), _const_spec((1, D_MODEL)),
                  _layer_spec((D_MODEL, IN_COLS_PAD), layer),
                  _const_spec((FNET_GROUP_DIM, 2 * FNET_GROUP_DIM)),
                  row(LANES), row(LANES),
                  _const_spec((1, MLA_Q_LORA)), _const_spec((1, MLA_KV_LORA)),
                  _layer_spec((MLA_Q_LORA, hw), layer), _layer_spec((MLA_KV_LORA, hw), layer),
                  _layer_spec((MLA_KV_LORA, hw), layer), _const_spec((1, hw))],
        out_specs=[row(n) for _, n in outs],
        out_shape=[jax.ShapeDtypeStruct(s, BF16) for s in outs],
        compiler_params=_params(1),
        name="inproj",
    )(x2, gain, w_in_p, dft_c, cos, sin, qn, kvn, wq, wk, wv, vb)


def _rope_table_kernel(pos_ref, invf_ref, cos_ref, sin_ref):
    ang = pos_ref[...] * invf_ref[...]
    cos_ref[...] = jnp.cos(ang)
    sin_ref[...] = jnp.sin(ang)


def _rope_table(posf, invf_lane):
    t = posf.shape[0]
    tm = TOKEN_TILE
    row = lambda n: pl.BlockSpec((tm, n), lambda i: (i, 0))
    return pl.pallas_call(
        _rope_table_kernel,
        grid=(t // tm,),
        in_specs=[row(1), _const_spec((1, LANES))],
        out_specs=[row(LANES)] * 2,
        out_shape=[jax.ShapeDtypeStruct((t, LANES), F32)] * 2,
        compiler_params=_params(1),
        name="rope_table",
    )(posf, invf_lane)


def _flash_kernel(q_ref, k_ref, v_ref, o_ref, *, tk, nk):
    tq = q_ref.shape[1]
    heads = [slice(h * HEAD_PAD, (h + 1) * HEAD_PAD) for h in range(2)]
    qs = [q_ref[0, :, sl] for sl in heads]

    def scores(i):
        return [_dot_nt(q, k_ref[0, i * tk:(i + 1) * tk, sl]) for q, sl in zip(qs, heads)]

    state = [(jnp.full((tq, 1), -jnp.inf, F32), jnp.zeros((tq, HEAD_PAD), F32))] * 2
    ss = scores(0)
    for i in range(nk):
        ss_next = scores(i + 1) if i + 1 < nk else None
        new = []
        for (m, acc), s, sl in zip(state, ss, heads):
            m_new = jnp.maximum(m, jnp.max(s, axis=-1, keepdims=True))
            p = jnp.exp2((s - m_new).astype(BF16))
            alpha = jnp.exp2(m - m_new)
            v = v_ref[0, i * tk:(i + 1) * tk, sl]
            new.append((m_new, alpha * acc + _dot(p, v)))
        state, ss = new, ss_next
    outs = [acc[:, :MLA_V] / acc[:, MLA_V:MLA_V + 1] for _, acc in state]
    o_ref[0] = jnp.concatenate(outs, axis=-1).astype(BF16)


def _flash(q, k, v):
    b, s, hw = q.shape
    tq, tk = FLASH_TQ, FLASH_TK
    pair = 2 * HEAD_PAD
    return pl.pallas_call(
        functools.partial(_flash_kernel, tk=tk, nk=s // tk),
        grid=(b, MLA_HEADS // 2, s // tq),
        in_specs=[pl.BlockSpec((1, tq, pair), lambda bi, hi, qi: (bi, qi, hi)),
                  pl.BlockSpec((1, s, pair), lambda bi, hi, qi: (bi, 0, hi)),
                  pl.BlockSpec((1, s, pair), lambda bi, hi, qi: (bi, 0, hi))],
        out_specs=pl.BlockSpec((1, tq, 2 * MLA_V), lambda bi, hi, qi: (bi, qi, hi)),
        out_shape=jax.ShapeDtypeStruct((b, s, MLA_HEADS * MLA_V), BF16),
        compiler_params=_params(3),
        name="mla_flash",
    )(q, k, v)


def _fft_a_kernel(xr_ref, xi_ref, m1_ref, ar_ref, ai_ref):
    xr = jnp.swapaxes(xr_ref[0], 0, 1)
    xi = jnp.swapaxes(xi_ref[0], 0, 1)
    out_r, out_i = [], []
    for j in range(xr.shape[0]):
        a = _dot(m1_ref[...], jnp.concatenate([xr[j], xi[j]], axis=0))
        out_r.append(a[:FFT_N1].astype(BF16))
        out_i.append(a[FFT_N1:].astype(BF16))
    ar_ref[0] = jnp.stack(out_r)
    ai_ref[0] = jnp.stack(out_i)


def _fft_a(wr, wi, m1):
    b, n1, n2, c = wr.shape
    nb2, tc = FFT_A_N2_BLOCK, FFT_A_COLS
    blk = pl.BlockSpec((1, n1, nb2, tc), lambda bi, ji, ci: (bi, 0, ji, ci))
    out_blk = pl.BlockSpec((1, nb2, n1, tc), lambda bi, ji, ci: (bi, ji, 0, ci))
    return pl.pallas_call(
        _fft_a_kernel,
        grid=(b, n2 // nb2, c // tc),
        in_specs=[blk, blk, _const_spec((2 * n1, 2 * n1))],
        out_specs=[out_blk, out_blk],
        out_shape=[jax.ShapeDtypeStruct((b, n2, n1, c), BF16)] * 2,
        compiler_params=_params(3),
        name="fft_a",
    )(wr, wi, m1)


def _fft_b_kernel(ar_ref, ai_ref, d_ref, o_ref, *, kb):
    ar = jnp.swapaxes(ar_ref[0], 0, 1)
    ai = jnp.swapaxes(ai_ref[0], 0, 1)
    res = []
    for j in range(kb):
        a = jnp.concatenate([ar[j], ai[j]], axis=0)
        res.append(_dot(d_ref[j], a).astype(BF16))
    o_ref[0] = jnp.swapaxes(jnp.stack(res), 0, 1)


def _fft_b(ar, ai, dmat):
    b, n2, n1, c = ar.shape
    kb = FFT_B_ROWS
    blk = pl.BlockSpec((1, n2, kb, c), lambda bi, ki: (bi, 0, ki, 0))
    return pl.pallas_call(
        functools.partial(_fft_b_kernel, kb=kb),
        grid=(b, n1 // kb),
        in_specs=[blk, blk, pl.BlockSpec((kb, n2, 2 * n2), lambda bi, ki: (ki, 0, 0))],
        out_specs=pl.BlockSpec((1, n2, kb, c), lambda bi, ki: (bi, 0, ki, 0)),
        out_shape=jax.ShapeDtypeStruct((b, n2, n1, c), BF16),
        compiler_params=_params(2),
        name="fft_b",
    )(ar, ai, dmat)


def _gqa_kernel(sink_ref, q_ref, kp_ref, kc_ref, kn_ref, vp_ref, vc_ref, vn_ref,
                pp_ref, pc_ref, pn_ref, o_ref, k_scr, v_scr, p_scr, *, seq, nq, slopes):
    j = pl.program_id(1)
    wb = WIN_BLOCK
    r = nq * wb
    k_scr[0:wb] = kp_ref[0]
    k_scr[wb:wb + r] = kc_ref[0]
    k_scr[wb + r:] = kn_ref[0]
    v_scr[0:wb] = vp_ref[0]
    v_scr[wb:wb + r] = vc_ref[0]
    v_scr[wb + r:] = vn_ref[0]
    p_scr[0] = pp_ref[0].astype(F32)
    p_scr[1:nq + 1] = pc_ref[...].astype(F32)
    p_scr[nq + 1] = pn_ref[0].astype(F32)

    kj = lax.broadcasted_iota(jnp.int32, (wb, 3 * wb), 1)
    qi = lax.broadcasted_iota(jnp.int32, (wb, 3 * wb), 0)
    band = jnp.abs(kj - wb - qi) <= WINDOW
    lane_kv = lax.broadcasted_iota(jnp.int32, (3 * wb, LANES), 1)
    low_half = lane_kv < GQA_HEAD_DIM
    ones_even = (lane_kv == GQA_HEAD_DIM).astype(BF16)
    ones_odd = (lane_kv == 0).astype(BF16)
    low_out = lax.broadcasted_iota(jnp.int32, (wb, LANES), 1) < GQA_HEAD_DIM

    def score_phase(a):
        rows = slice(a * wb, (a + 3) * wb)
        pc = p_scr[a + 1]
        pq = jnp.broadcast_to(pc, (wb, wb)).T
        dist = jnp.concatenate(
            [jnp.abs(pq - p_scr[a]), jnp.abs(pq - pc), jnp.abs(pq - p_scr[a + 2])], axis=1)
        key_idx = (j * nq + a) * wb - wb + kj
        mask = band & (key_idx >= 0) & (key_idx < seq)
        nd = jnp.where(mask, -dist, -jnp.inf)
        k_half = []
        for hk in range(GQA_KV_HEADS):
            kd = k_scr[rows, hk * LANES:(hk + 1) * LANES]
            zero = jnp.zeros_like(kd)
            k_half.append((jnp.where(low_half, kd, zero), jnp.where(low_half, zero, kd)))
        scores = []
        for hq in range(GQA_Q_HEADS):
            pair, par, hk = hq // 2, hq % 2, hq // GQA_GROUP
            qp = q_ref[0, a * wb:(a + 1) * wb, pair * LANES:(pair + 1) * LANES]
            scores.append(_dot_nt(qp, k_half[hk][par]) + (slopes[hq] * LOG2E) * nd)
        return scores

    def value_phase(a, scores):
        rows = slice(a * wb, (a + 3) * wb)
        v_half = []
        for hk in range(GQA_KV_HEADS):
            vd = v_scr[rows, hk * LANES:(hk + 1) * LANES]
            v_half.append((jnp.where(low_half, vd, ones_even), jnp.where(low_half, ones_odd, vd)))
        probs, sink_terms = [], []
        for hq, s in enumerate(scores):
            sink = sink_ref[hq] * LOG2E
            m = jnp.maximum(jnp.max(s, axis=-1, keepdims=True), sink)
            probs.append(jnp.exp2(s - m).astype(BF16))
            sink_terms.append(jnp.exp2(sink - m))
        outs = []
        for pair in range(GQA_Q_HEADS // 2):
            hk = (2 * pair) // GQA_GROUP
            even, odd = 2 * pair, 2 * pair + 1
            acc_e = _dot(probs[even], v_half[hk][0])
            acc_o = _dot(probs[odd], v_half[hk][1])
            r_e = 1.0 / (acc_e[:, GQA_HEAD_DIM:GQA_HEAD_DIM + 1] + sink_terms[even])
            r_o = 1.0 / (acc_o[:, 0:1] + sink_terms[odd])
            outs.append(jnp.where(low_out, acc_e * r_e, acc_o * r_o))
        o_ref[0, a * wb:(a + 1) * wb, :] = jnp.concatenate(outs, axis=-1).astype(BF16)

    scores = score_phase(0)
    for a in range(nq):
        scores_next = score_phase(a + 1) if a + 1 < nq else None
        value_phase(a, scores)
        scores = scores_next


def _gqa(qkv, pos_blocks, sink, slopes):
    b, s, _ = qkv.shape
    wb = WIN_BLOCK
    nb = s // wb
    nq = GQA_NQ
    r = nq * wb
    prev = lambda j: jnp.maximum(j * nq - 1, 0)
    nxt = lambda j: jnp.minimum((j + 1) * nq, nb - 1)
    edge = lambda lane_blk, f: pl.BlockSpec((1, wb, 2 * LANES), lambda bi, j: (bi, f(j), lane_blk))
    own = lambda lane_blk: pl.BlockSpec((1, r, 2 * LANES), lambda bi, j: (bi, j, lane_blk))
    pedge = lambda f: pl.BlockSpec((1, 1, wb), lambda bi, j: (bi * nb + f(j), 0, 0))
    qw = GQA_Q_HEADS * GQA_HEAD_DIM
    return pl.pallas_call(
        functools.partial(_gqa_kernel, seq=s, nq=nq, slopes=slopes),
        grid=(b, nb // nq),
        in_specs=[pl.BlockSpec(memory_space=pltpu.SMEM),
                  pl.BlockSpec((1, r, qw), lambda bi, j: (bi, j, 0)),
                  edge(2, prev), own(2), edge(2, nxt),
                  edge(3, prev), own(3), edge(3, nxt),
                  pedge(prev),
                  pl.BlockSpec((nq, 1, wb), lambda bi, j: (bi * (nb // nq) + j, 0, 0)),
                  pedge(nxt)],
        out_specs=pl.BlockSpec((1, r, qw), lambda bi, j: (bi, j, 0)),
        out_shape=jax.ShapeDtypeStruct((b, s, qw), BF16),
        scratch_shapes=[pltpu.VMEM((r + 2 * wb, 2 * LANES), BF16),
                        pltpu.VMEM((r + 2 * wb, 2 * LANES), BF16),
                        pltpu.VMEM((nq + 2, 1, wb), F32)],
        compiler_params=_params(2),
        name="gqa_window",
    )(sink, qkv, qkv, qkv, qkv, qkv, qkv, qkv, pos_blocks, pos_blocks, pos_blocks)


def _token_kernel(x_ref, oa_ref, ob_ref, oc_ref, gate_ref, p_ref,
                  wa_ref, wb_ref, wc_ref, wo_ref, g_mix_ref,
                  g_pre_ref, wg_ref, wu_ref, wd_ref, g_post_ref,
                  wp_ref, wpg_ref, g_ple_ref, out_ref):
    d = D_MODEL
    merged = (gate_ref[:, 0:d].astype(F32) * _dot(oa_ref[...], wa_ref[...])
              + gate_ref[:, d:2 * d].astype(F32) * _dot(ob_ref[...], wb_ref[...])
              + gate_ref[:, 2 * d:3 * d].astype(F32) * _dot(oc_ref[...], wc_ref[...]))
    x = x_ref[...] + _rms(_dot(merged.astype(BF16), wo_ref[...]), g_mix_ref[...])

    h = _rms(x, g_pre_ref[...]).astype(BF16)
    ff = jnp.zeros(x.shape, F32)
    for c in range(FFN_DIM // FFN_CHUNK):
        sl = slice(c * FFN_CHUNK, (c + 1) * FFN_CHUNK)
        a = jax.nn.silu(_dot(h, wg_ref[:, sl])) * _dot(h, wu_ref[:, sl])
        ff = ff + _dot(a.astype(BF16), wd_ref[sl, :])
    x = x + _rms(ff, g_post_ref[...])

    e = _dot(p_ref[...].astype(BF16), wp_ref[...]) * jax.nn.sigmoid(
        _dot(x.astype(BF16), wpg_ref[...]))
    out_ref[...] = x + _rms(e, g_ple_ref[...])


def _token_tail(x2, oa, ob, oc, gates, p2, wa, wb, wc, wo, g_mix,
                g_pre, wg, wu, wd, g_post, wp, wpg, g_ple, layer):
    t = x2.shape[0]
    tm = TOKEN_TILE
    d = D_MODEL
    row = lambda n: pl.BlockSpec((tm, n), lambda i: (i, 0))
    tiles_per_seq = p2.shape[2] // tm
    p_row = pl.BlockSpec((None, None, tm, PLE_DIM),
                         lambda i: (layer, i // tiles_per_seq, i % tiles_per_seq, 0))
    vec = _const_spec((1, d))
    w = lambda a: _layer_spec(a.shape[1:], layer)
    return pl.pallas_call(
        _token_kernel,
        grid=(t // tm,),
        in_specs=[row(d), row(oa.shape[1]), row(d), row(oc.shape[1]), row(3 * d), p_row,
                  w(wa), w(wb), w(wc), w(wo), vec,
                  vec, w(wg), w(wu), w(wd), vec,
                  w(wp), w(wpg), vec],
        out_specs=row(d),
        out_shape=jax.ShapeDtypeStruct((t, d), F32),
        compiler_params=_params(1),
        name="token_tail",
    )(x2, oa, ob, oc, gates, p2, wa, wb, wc, wo, g_mix, g_pre, wg, wu, wd, g_post, wp, wpg, g_ple)


def _cos_sin(num, den):
    ang = (num % den).astype(F32) * (2.0 * math.pi / den)
    return jnp.cos(ang), jnp.sin(ang)


def _dft_tables(seq):
    gd = FNET_GROUP_DIM
    jk = jnp.arange(gd, dtype=jnp.int32)
    c, s = _cos_sin(jk[:, None] * jk[None, :], gd)
    chan = (jnp.concatenate([c, -s], axis=1) * gd ** -0.5).astype(BF16)

    n1, n2 = FFT_N1, seq // FFT_N1
    a = jnp.arange(n1, dtype=jnp.int32)
    c1, s1 = _cos_sin(a[:, None] * a[None, :], n1)
    m1 = (jnp.concatenate([jnp.concatenate([c1, s1], axis=1),
                           jnp.concatenate([-s1, c1], axis=1)], axis=0) * n1 ** -0.5).astype(BF16)

    k1 = jnp.arange(n1, dtype=jnp.int32)[:, None]
    k2 = jnp.arange(n2, dtype=jnp.int32)[:, None]
    nn = jnp.arange(n2, dtype=jnp.int32)[None, :]
    ca, sa = _cos_sin(k1 * nn, seq)
    cb, sb = _cos_sin(k2 * nn, n2)
    cd = ca[:, None, :] * cb[None] - sa[:, None, :] * sb[None]
    sd = sa[:, None, :] * cb[None] + ca[:, None, :] * sb[None]
    dmat = (jnp.concatenate([cd, sd], axis=2) * n2 ** -0.5).astype(BF16)
    return chan, m1, dmat


def _head_blocks(w):
    depth, rows, heads, used = w.shape
    pad = jnp.zeros((depth, rows, heads, HEAD_PAD - used), w.dtype)
    return jnp.concatenate([w, pad], axis=3).reshape(depth, rows, heads * HEAD_PAD).astype(BF16)


def _prep_weights(w_in, w_uq, w_ukv):
    depth, d = w_in.shape[:2]
    qc = (w_in[..., 544:1056] * ((GQA_HEAD_DIM ** -0.5) * LOG2E)).astype(BF16)
    wi = w_in.astype(BF16)
    hd = GQA_HEAD_DIM
    dup = lambda w: jnp.concatenate([w[..., :hd], w[..., :hd], w[..., hd:], w[..., hd:]], axis=-1)
    zcols = lambda n: jnp.zeros((depth, d, n), BF16)
    w_in_p = jnp.concatenate(
        [wi[..., :512], zcols(MLA_NOPE), wi[..., 512:544], zcols(HEAD_PAD - MLA_QK), qc,
         dup(wi[..., 1056:1184]), dup(wi[..., 1184:1312]), wi[..., 1312:]], axis=-1)

    uq = w_uq.reshape(depth, MLA_Q_LORA, MLA_HEADS, MLA_QK) * ((MLA_QK ** -0.5) * LOG2E)
    wq = _head_blocks(uq)
    ukv = w_ukv.reshape(depth, MLA_KV_LORA, MLA_HEADS, MLA_NOPE + MLA_V)
    wk = _head_blocks(ukv[..., :MLA_NOPE])
    wv = _head_blocks(ukv[..., MLA_NOPE:])
    return w_in_p, wq, wk, wv


def kernel(x, p, positions, norm_mix_pre, w_in, mla_q_norm, w_uq, mla_kv_norm, w_ukv, gqa_sink,
           w_branch_a, w_branch_b, w_branch_c, w_out, norm_mix_post, norm_ffn_pre, w_ffn_gate,
           w_ffn_up, w_ffn_down, norm_ffn_post, w_ple_proj, w_ple_gate, norm_ple):
    b, s, d = x.shape
    depth = w_in.shape[0]
    t = b * s
    n1, n2 = FFT_N1, s // FFT_N1
    assert d == D_MODEL and s % TOKEN_TILE == 0 and s % FLASH_TQ == 0 and s % FLASH_TK == 0
    assert n2 % FFT_A_N2_BLOCK == 0 and d % FFT_A_COLS == 0 and (s // WIN_BLOCK) % GQA_NQ == 0

    chan, m1, dmat = _dft_tables(s)
    half = MLA_ROPE // 2
    inv_freq = ROPE_THETA ** (-jnp.arange(half, dtype=F32) / half)
    invf_lane = jnp.zeros((1, LANES), F32)
    invf_lane = invf_lane.at[0, MLA_NOPE:MLA_NOPE + half].set(inv_freq)
    invf_lane = invf_lane.at[0, MLA_NOPE + half:MLA_NOPE + MLA_ROPE].set(inv_freq)
    posf = positions.reshape(t, 1).astype(F32)
    pos_blocks = positions.reshape(b * (s // WIN_BLOCK), 1, WIN_BLOCK)
    vbias = jnp.zeros((MLA_HEADS, HEAD_PAD), F32).at[:, MLA_V].set(1.0).reshape(1, -1)
    slopes = tuple(float(v) for v in
                   2.0 ** (-8.0 * (np.arange(GQA_Q_HEADS, dtype=np.float32) + 1.0) / GQA_Q_HEADS))
    r1 = lambda v: v.reshape(1, -1)
    tail_w = [w.astype(BF16) for w in (w_branch_a, w_branch_b, w_branch_c, w_out, w_ffn_gate,
                                       w_ffn_up, w_ffn_down, w_ple_proj, w_ple_gate)]
    wa_all, wb_all, wc_all, wo_all, wg_all, wu_all, wd_all, wp_all, wpg_all = tail_w
    w_in_p, wq, wk, wv = _prep_weights(w_in, w_uq, w_ukv)

    rope_cos, rope_sin = _rope_table(posf, invf_lane)

    x2 = x.reshape(t, d)
    for i in range(depth):
        qkv, gates, wr, wi, q, k, v = _inproj(
            x2, r1(norm_mix_pre[i]), w_in_p, chan, rope_cos, rope_sin,
            r1(mla_q_norm[i]), r1(mla_kv_norm[i]), wq, wk, wv, vbias, i)
        hw = MLA_HEADS * HEAD_PAD
        o_a = _flash(q.reshape(b, s, hw), k.reshape(b, s, hw), v.reshape(b, s, hw))

        ar, ai = _fft_a(wr.reshape(b, n1, n2, d), wi.reshape(b, n1, n2, d), m1)
        o_b = _fft_b(ar, ai, dmat)

        o_c = _gqa(qkv.reshape(b, s, -1), pos_blocks, gqa_sink[i], slopes)

        x2 = _token_tail(
            x2, o_a.reshape(t, -1), o_b.reshape(t, d), o_c.reshape(t, -1), gates,
            p, wa_all, wb_all, wc_all, wo_all,
            r1(norm_mix_post[i]), r1(norm_ffn_pre[i]), wg_all, wu_all,
            wd_all, r1(norm_ffn_post[i]), wp_all, wpg_all,
            r1(norm_ple[i]), i)
    return x2.reshape(b, s, d)
```

```python
import functools
import math

import numpy as np
import jax
import jax.numpy as jnp
from jax import lax
from jax.experimental import pallas as pl
from jax.experimental.pallas import tpu as pltpu

F32 = jnp.float32
BF16 = jnp.bfloat16

D_MODEL = 1024
PLE_DIM = 256
MLA_HEADS = 8
MLA_NOPE = 64
MLA_ROPE = 32
MLA_V = 64
MLA_QK = MLA_NOPE + MLA_ROPE
MLA_Q_LORA = 384
MLA_KV_LORA = 128
ROPE_THETA = 10000.0
FNET_GROUPS = 4
FNET_GROUP_DIM = D_MODEL // FNET_GROUPS
GQA_Q_HEADS = 8
GQA_KV_HEADS = 2
GQA_GROUP = GQA_Q_HEADS // GQA_KV_HEADS
GQA_HEAD_DIM = 64
WINDOW = 128
WIN_BLOCK = 128
FFN_DIM = 2816
RMS_EPS = 1e-6

LANES = 128
HEAD_PAD = 128
FFT_N1 = 128
VMEM_LIMIT = 56 * 1024 * 1024

TOKEN_TILE = 512
ROPE_PACK = LANES // (MLA_ROPE // 2)
FLASH_TQ = 1024
FLASH_TK = 1024
GQA_NQ = 8
FFT_A_N2_BLOCK = 16
FFT_A_COLS = 512
FFT_B_ROWS = 16
FFN_CHUNK = 256
LOG2E = 1.4426950408889634

SEG_CQ = (0, 384)
SEG_CKV = (384, 640)
SEG_QKV = (640, 1664)
SEG_GATE = (1664, 4736)
IN_COLS_PAD = 4736


def _params(n_parallel_axes):
    return pltpu.CompilerParams(
        dimension_semantics=("arbitrary",) * n_parallel_axes,
        vmem_limit_bytes=VMEM_LIMIT)


def _const_spec(shape):
    zeros = (0,) * len(shape)
    return pl.BlockSpec(shape, lambda *_: zeros, pipeline_mode=pl.Buffered(1))


def _layer_spec(shape, layer):
    zeros = (0,) * len(shape)
    return pl.BlockSpec((None,) + tuple(shape), lambda *_: (layer,) + zeros,
                        pipeline_mode=pl.Buffered(1))


def _rms(x, gain):
    ms = jnp.mean(x * x, axis=-1, keepdims=True)
    return x * lax.rsqrt(ms + RMS_EPS) * gain


def _dot(a, b):
    return jnp.dot(a, b, preferred_element_type=F32)


def _dot_nt(a, b):
    return lax.dot_general(a, b, (((1,), (1,)), ((), ())), preferred_element_type=F32)


def _inproj_kernel(x_ref, gain_ref, w_ref, dft_ref, cos_ref, sin_ref, qn_ref, kvn_ref,
                   wq_ref, wk_ref, wv_ref, vb_ref,
                   qkv_ref, gate_ref, wr_ref, wi_ref, q_ref, k_ref, v_ref):
    h = _rms(x_ref[...], gain_ref[...]).astype(BF16)
    cq = _dot(h, w_ref[:, SEG_CQ[0]:SEG_CQ[1]])
    ckv = _dot(h, w_ref[:, SEG_CKV[0]:SEG_CKV[1]])
    qkv_ref[...] = _dot(h, w_ref[:, SEG_QKV[0]:SEG_QKV[1]]).astype(BF16)
    for c in range(3):
        lo = SEG_GATE[0] + c * D_MODEL
        g = _dot(h, w_ref[:, lo:lo + D_MODEL])
        gate_ref[:, c * D_MODEL:(c + 1) * D_MODEL] = jax.nn.sigmoid(g).astype(BF16)

    cqn = _rms(cq, qn_ref[...]).astype(BF16)
    ckvn = _rms(ckv[:, :MLA_KV_LORA], kvn_ref[...]).astype(BF16)
    cos = cos_ref[...]
    sin = sin_ref[...]
    half = MLA_ROPE // 2
    lane = lax.broadcasted_iota(jnp.int32, cos.shape, 1)
    second = lane >= MLA_NOPE + half
    sin_up = jnp.where(second, sin, 0.0)
    sin_dn = jnp.where(second, 0.0, -sin)

    def rope(x):
        return (x * cos + pltpu.roll(x, half, 1) * sin_up
                + pltpu.roll(x, HEAD_PAD - half, 1) * sin_dn)

    kr = rope(ckv[:, MLA_KV_LORA:])
    for hd in range(MLA_HEADS):
        sl = slice(hd * HEAD_PAD, (hd + 1) * HEAD_PAD)
        q_ref[:, sl] = rope(_dot(cqn, wq_ref[:, sl])).astype(BF16)
        k_ref[:, sl] = (_dot(ckvn, wk_ref[:, sl]) + kr).astype(BF16)
        v_ref[:, sl] = (_dot(ckvn, wv_ref[:, sl]) + vb_ref[:, sl]).astype(BF16)

    gd = FNET_GROUP_DIM
    for g in range(FNET_GROUPS):
        yz = _dot(h[:, g * gd:(g + 1) * gd], dft_ref[...])
        wr_ref[:, g * gd:(g + 1) * gd] = yz[:, :gd].astype(BF16)
        wi_ref[:, g * gd:(g + 1) * gd] = yz[:, gd:].astype(BF16)


def _inproj(x2, gain, w_in_p, dft_c, cos, sin, qn, kvn, wq, wk, wv, vb, layer):
    t = x2.shape[0]
    tm = TOKEN_TILE
    hw = MLA_HEADS * HEAD_PAD
    row = lambda n: pl.BlockSpec((tm, n), lambda i: (i, 0))
    outs = [(t, SEG_QKV[1] - SEG_QKV[0]), (t, 3 * D_MODEL), (t, D_MODEL), (t, D_MODEL),
            (t, hw), (t, hw), (t, hw)]
    return pl.pallas_call(
        _inproj_kernel,
        grid=(t // tm,),
        in_specs=[row(D_MODEL), _const_spec((1, D_MODEL)),
                  _layer_spec((D_MODEL, IN_COLS_PAD), layer),
                  _const_spec((FNET_GROUP_DIM, 2 * FNET_GROUP_DIM)),
                  row(LANES), row(LANES),
                  _const_spec((1, MLA_Q_LORA)), _const_spec((1, MLA_KV_LORA)),
                  _layer_spec((MLA_Q_LORA, hw), layer), _layer_spec((MLA_KV_LORA, hw), layer),
                  _layer_spec((MLA_KV_LORA, hw), layer), _const_spec((1, hw))],
        out_specs=[row(n) for _, n in outs],
        out_shape=[jax.ShapeDtypeStruct(s, BF16) for s in outs],
        compiler_params=_params(1),
        name="inproj",
    )(x2, gain, w_in_p, dft_c, cos, sin, qn, kvn, wq, wk, wv, vb)


def _rope_table_kernel(pos_ref, invf_ref, place_ref, cos_ref, sin_ref):
    ang = pos_ref[...] * invf_ref[...]
    per_row = ROPE_PACK
    lane = lax.broadcasted_iota(jnp.int32, (pos_ref.shape[0] * per_row, LANES), 1)
    on_rope = (lane >= MLA_NOPE) & (lane < MLA_QK)

    def spread(dense):
        hi = dense.astype(BF16)
        lo = (dense - hi.astype(F32)).astype(BF16)
        parts = [_dot(hi, place_ref[u]) + _dot(lo, place_ref[u]) for u in range(per_row)]
        return jnp.swapaxes(jnp.stack(parts), 0, 1).reshape(lane.shape)

    cos_ref[...] = jnp.where(on_rope, spread(jnp.cos(ang)), 1.0)
    sin_ref[...] = spread(jnp.sin(ang))


def _rope_table(pos_dense, invf_dense, place):
    rows = pos_dense.shape[0]
    t = rows * ROPE_PACK
    tm = TOKEN_TILE
    return pl.pallas_call(
        _rope_table_kernel,
        grid=(t // tm,),
        in_specs=[pl.BlockSpec((tm // ROPE_PACK, LANES), lambda i: (i, 0)),
                  _const_spec((1, LANES)), _const_spec(place.shape)],
        out_specs=[pl.BlockSpec((tm, LANES), lambda i: (i, 0))] * 2,
        out_shape=[jax.ShapeDtypeStruct((t, LANES), F32)] * 2,
        compiler_params=_params(1),
        name="rope_table",
    )(pos_dense, invf_dense, place)


def _flash_kernel(q_ref, k_ref, v_ref, o_ref, *, tk, nk):
    tq = q_ref.shape[1]
    heads = [slice(h * HEAD_PAD, (h + 1) * HEAD_PAD) for h in range(2)]
    qs = [q_ref[0, :, sl] for sl in heads]

    def scores(i):
        return [_dot_nt(q, k_ref[0, i * tk:(i + 1) * tk, sl]) for q, sl in zip(qs, heads)]

    state = [(jnp.full((tq, 1), -jnp.inf, F32), jnp.zeros((tq, HEAD_PAD), F32))] * 2
    ss = scores(0)
    for i in range(nk):
        ss_next = scores(i + 1) if i + 1 < nk else None
        new = []
        for (m, acc), s, sl in zip(state, ss, heads):
            m_new = jnp.maximum(m, jnp.max(s, axis=-1, keepdims=True))
            p = jnp.exp2((s - m_new).astype(BF16))
            alpha = jnp.exp2(m - m_new)
            v = v_ref[0, i * tk:(i + 1) * tk, sl]
            new.append((m_new, alpha * acc + _dot(p, v)))
        state, ss = new, ss_next
    outs = [acc[:, :MLA_V] / acc[:, MLA_V:MLA_V + 1] for _, acc in state]
    o_ref[0] = jnp.concatenate(outs, axis=-1).astype(BF16)


def _flash(q, k, v):
    b, s, hw = q.shape
    tq, tk = FLASH_TQ, FLASH_TK
    pair = 2 * HEAD_PAD
    return pl.pallas_call(
        functools.partial(_flash_kernel, tk=tk, nk=s // tk),
        grid=(b, MLA_HEADS // 2, s // tq),
        in_specs=[pl.BlockSpec((1, tq, pair), lambda bi, hi, qi: (bi, qi, hi)),
                  pl.BlockSpec((1, s, pair), lambda bi, hi, qi: (bi, 0, hi)),
                  pl.BlockSpec((1, s, pair), lambda bi, hi, qi: (bi, 0, hi))],
        out_specs=pl.BlockSpec((1, tq, 2 * MLA_V), lambda bi, hi, qi: (bi, qi, hi)),
        out_shape=jax.ShapeDtypeStruct((b, s, MLA_HEADS * MLA_V), BF16),
        compiler_params=_params(3),
        name="mla_flash",
    )(q, k, v)


def _fft_a_kernel(xr_ref, xi_ref, m1_ref, ar_ref, ai_ref):
    xr = jnp.swapaxes(xr_ref[0], 0, 1)
    xi = jnp.swapaxes(xi_ref[0], 0, 1)
    out_r, out_i = [], []
    for j in range(xr.shape[0]):
        a = _dot(m1_ref[...], jnp.concatenate([xr[j], xi[j]], axis=0))
        out_r.append(a[:FFT_N1].astype(BF16))
        out_i.append(a[FFT_N1:].astype(BF16))
    ar_ref[0] = jnp.stack(out_r)
    ai_ref[0] = jnp.stack(out_i)


def _fft_a(wr, wi, m1):
    b, n1, n2, c = wr.shape
    nb2, tc = FFT_A_N2_BLOCK, FFT_A_COLS
    blk = pl.BlockSpec((1, n1, nb2, tc), lambda bi, ji, ci: (bi, 0, ji, ci))
    out_blk = pl.BlockSpec((1, nb2, n1, tc), lambda bi, ji, ci: (bi, ji, 0, ci))
    return pl.pallas_call(
        _fft_a_kernel,
        grid=(b, n2 // nb2, c // tc),
        in_specs=[blk, blk, _const_spec((2 * n1, 2 * n1))],
        out_specs=[out_blk, out_blk],
        out_shape=[jax.ShapeDtypeStruct((b, n2, n1, c), BF16)] * 2,
        compiler_params=_params(3),
        name="fft_a",
    )(wr, wi, m1)


def _fft_b_kernel(ar_ref, ai_ref, d_ref, o_ref, *, kb):
    ar = jnp.swapaxes(ar_ref[0], 0, 1)
    ai = jnp.swapaxes(ai_ref[0], 0, 1)
    res = []
    for j in range(kb):
        a = jnp.concatenate([ar[j], ai[j]], axis=0)
        res.append(_dot(d_ref[j], a).astype(BF16))
    o_ref[0] = jnp.swapaxes(jnp.stack(res), 0, 1)


def _fft_b(ar, ai, dmat):
    b, n2, n1, c = ar.shape
    kb = FFT_B_ROWS
    blk = pl.BlockSpec((1, n2, kb, c), lambda bi, ki: (bi, 0, ki, 0))
    return pl.pallas_call(
        functools.partial(_fft_b_kernel, kb=kb),
        grid=(b, n1 // kb),
        in_specs=[blk, blk, pl.BlockSpec((kb, n2, 2 * n2), lambda bi, ki: (ki, 0, 0))],
        out_specs=pl.BlockSpec((1, n2, kb, c), lambda bi, ki: (bi, 0, ki, 0)),
        out_shape=jax.ShapeDtypeStruct((b, n2, n1, c), BF16),
        compiler_params=_params(2),
        name="fft_b",
    )(ar, ai, dmat)


def _gqa_kernel(sink_ref, q_ref, kp_ref, kc_ref, kn_ref, vp_ref, vc_ref, vn_ref,
                pp_ref, pc_ref, pn_ref, o_ref, k_scr, v_scr, p_scr, *, seq, nq, slopes):
    j = pl.program_id(1)
    wb = WIN_BLOCK
    r = nq * wb
    k_scr[0:wb] = kp_ref[0]
    k_scr[wb:wb + r] = kc_ref[0]
    k_scr[wb + r:] = kn_ref[0]
    v_scr[0:wb] = vp_ref[0]
    v_scr[wb:wb + r] = vc_ref[0]
    v_scr[wb + r:] = vn_ref[0]
    p_scr[0] = pp_ref[0].astype(F32)
    p_scr[1:nq + 1] = pc_ref[...].astype(F32)
    p_scr[nq + 1] = pn_ref[0].astype(F32)

    kj = lax.broadcasted_iota(jnp.int32, (wb, 3 * wb), 1)
    qi = lax.broadcasted_iota(jnp.int32, (wb, 3 * wb), 0)
    band = jnp.abs(kj - wb - qi) <= WINDOW
    lane_kv = lax.broadcasted_iota(jnp.int32, (3 * wb, LANES), 1)
    low_half = lane_kv < GQA_HEAD_DIM
    ones_even = (lane_kv == GQA_HEAD_DIM).astype(BF16)
    ones_odd = (lane_kv == 0).astype(BF16)
    low_out = lax.broadcasted_iota(jnp.int32, (wb, LANES), 1) < GQA_HEAD_DIM

    def score_phase(a):
        rows = slice(a * wb, (a + 3) * wb)
        pc = p_scr[a + 1]
        pq = jnp.broadcast_to(pc, (wb, wb)).T
        dist = jnp.concatenate(
            [jnp.abs(pq - p_scr[a]), jnp.abs(pq - pc), jnp.abs(pq - p_scr[a + 2])], axis=1)
        key_idx = (j * nq + a) * wb - wb + kj
        mask = band & (key_idx >= 0) & (key_idx < seq)
        nd = jnp.where(mask, -dist, -jnp.inf)
        k_half = []
        for hk in range(GQA_KV_HEADS):
            kd = k_scr[rows, hk * LANES:(hk + 1) * LANES]
            zero = jnp.zeros_like(kd)
            k_half.append((jnp.where(low_half, kd, zero), jnp.where(low_half, zero, kd)))
        scores = []
        for hq in range(GQA_Q_HEADS):
            pair, par, hk = hq // 2, hq % 2, hq // GQA_GROUP
            qp = q_ref[0, a * wb:(a + 1) * wb, pair * LANES:(pair + 1) * LANES]
            scores.append(_dot_nt(qp, k_half[hk][par]) + (slopes[hq] * LOG2E) * nd)
        return scores

    def value_phase(a, scores):
        rows = slice(a * wb, (a + 3) * wb)
        v_half = []
        for hk in range(GQA_KV_HEADS):
            vd = v_scr[rows, hk * LANES:(hk + 1) * LANES]
            v_half.append((jnp.where(low_half, vd, ones_even), jnp.where(low_half, ones_odd, vd)))
        probs, sink_terms = [], []
        for hq, s in enumerate(scores):
            sink = sink_ref[hq] * LOG2E
            m = jnp.maximum(jnp.max(s, axis=-1, keepdims=True), sink)
            probs.append(jnp.exp2(s - m).astype(BF16))
            sink_terms.append(jnp.exp2(sink - m))
        outs = []
        for pair in range(GQA_Q_HEADS // 2):
            hk = (2 * pair) // GQA_GROUP
            even, odd = 2 * pair, 2 * pair + 1
            acc_e = _dot(probs[even], v_half[hk][0])
            acc_o = _dot(probs[odd], v_half[hk][1])
            r_e = 1.0 / (acc_e[:, GQA_HEAD_DIM:GQA_HEAD_DIM + 1] + sink_terms[even])
            r_o = 1.0 / (acc_o[:, 0:1] + sink_terms[odd])
            outs.append(jnp.where(low_out, acc_e * r_e, acc_o * r_o))
        o_ref[0, a * wb:(a + 1) * wb, :] = jnp.concatenate(outs, axis=-1).astype(BF16)

    scores = score_phase(0)
    for a in range(nq):
        scores_next = score_phase(a + 1) if a + 1 < nq else None
        value_phase(a, scores)
        scores = scores_next


def _gqa(qkv, pos_blocks, sink, slopes):
    b, s, _ = qkv.shape
    wb = WIN_BLOCK
    nb = s // wb
    nq = GQA_NQ
    r = nq * wb
    prev = lambda j: jnp.maximum(j * nq - 1, 0)
    nxt = lambda j: jnp.minimum((j + 1) * nq, nb - 1)
    edge = lambda lane_blk, f: pl.BlockSpec((1, wb, 2 * LANES), lambda bi, j: (bi, f(j), lane_blk))
    own = lambda lane_blk: pl.BlockSpec((1, r, 2 * LANES), lambda bi, j: (bi, j, lane_blk))
    pedge = lambda f: pl.BlockSpec((1, 1, wb), lambda bi, j: (bi * nb + f(j), 0, 0))
    qw = GQA_Q_HEADS * GQA_HEAD_DIM
    return pl.pallas_call(
        functools.partial(_gqa_kernel, seq=s, nq=nq, slopes=slopes),
        grid=(b, nb // nq),
        in_specs=[pl.BlockSpec(memory_space=pltpu.SMEM),
                  pl.BlockSpec((1, r, qw), lambda bi, j: (bi, j, 0)),
                  edge(2, prev), own(2), edge(2, nxt),
                  edge(3, prev), own(3), edge(3, nxt),
                  pedge(prev),
                  pl.BlockSpec((nq, 1, wb), lambda bi, j: (bi * (nb // nq) + j, 0, 0)),
                  pedge(nxt)],
        out_specs=pl.BlockSpec((1, r, qw), lambda bi, j: (bi, j, 0)),
        out_shape=jax.ShapeDtypeStruct((b, s, qw), BF16),
        scratch_shapes=[pltpu.VMEM((r + 2 * wb, 2 * LANES), BF16),
                        pltpu.VMEM((r + 2 * wb, 2 * LANES), BF16),
                        pltpu.VMEM((nq + 2, 1, wb), F32)],
        compiler_params=_params(2),
        name="gqa_window",
    )(sink, qkv, qkv, qkv, qkv, qkv, qkv, qkv, pos_blocks, pos_blocks, pos_blocks)


def _token_kernel(x_ref, oa_ref, ob_ref, oc_ref, gate_ref, p_ref,
                  wa_ref, wb_ref, wc_ref, wo_ref, g_mix_ref,
                  g_pre_ref, wg_ref, wu_ref, wd_ref, g_post_ref,
                  wp_ref, wpg_ref, g_ple_ref, out_ref):
    d = D_MODEL
    merged = (gate_ref[:, 0:d].astype(F32) * _dot(oa_ref[...], wa_ref[...])
              + gate_ref[:, d:2 * d].astype(F32) * _dot(ob_ref[...], wb_ref[...])
              + gate_ref[:, 2 * d:3 * d].astype(F32) * _dot(oc_ref[...], wc_ref[...]))
    x = x_ref[...] + _rms(_dot(merged.astype(BF16), wo_ref[...]), g_mix_ref[...])

    h = _rms(x, g_pre_ref[...]).astype(BF16)
    ff = jnp.zeros(x.shape, F32)
    for c in range(FFN_DIM // FFN_CHUNK):
        sl = slice(c * FFN_CHUNK, (c + 1) * FFN_CHUNK)
        a = jax.nn.silu(_dot(h, wg_ref[:, sl])) * _dot(h, wu_ref[:, sl])
        ff = ff + _dot(a.astype(BF16), wd_ref[sl, :])
    x = x + _rms(ff, g_post_ref[...])

    e = _dot(p_ref[...].astype(BF16), wp_ref[...]) * jax.nn.sigmoid(
        _dot(x.astype(BF16), wpg_ref[...]))
    out_ref[...] = x + _rms(e, g_ple_ref[...])


def _token_tail(x2, oa, ob, oc, gates, p2, wa, wb, wc, wo, g_mix,
                g_pre, wg, wu, wd, g_post, wp, wpg, g_ple, layer):
    t = x2.shape[0]
    tm = TOKEN_TILE
    d = D_MODEL
    row = lambda n: pl.BlockSpec((tm, n), lambda i: (i, 0))
    tiles_per_seq = p2.shape[2] // tm
    p_row = pl.BlockSpec((None, None, tm, PLE_DIM),
                         lambda i: (layer, i // tiles_per_seq, i % tiles_per_seq, 0))
    vec = _const_spec((1, d))
    w = lambda a: _layer_spec(a.shape[1:], layer)
    return pl.pallas_call(
        _token_kernel,
        grid=(t // tm,),
        in_specs=[row(d), row(oa.shape[1]), row(d), row(oc.shape[1]), row(3 * d), p_row,
                  w(wa), w(wb), w(wc), w(wo), vec,
                  vec, w(wg), w(wu), w(wd), vec,
                  w(wp), w(wpg), vec],
        out_specs=row(d),
        out_shape=jax.ShapeDtypeStruct((t, d), F32),
        compiler_params=_params(1),
        name="token_tail",
    )(x2, oa, ob, oc, gates, p2, wa, wb, wc, wo, g_mix, g_pre, wg, wu, wd, g_post, wp, wpg, g_ple)


def _cos_sin(num, den):
    ang = (num % den).astype(F32) * (2.0 * math.pi / den)
    return jnp.cos(ang), jnp.sin(ang)


def _dft_tables(seq):
    gd = FNET_GROUP_DIM
    jk = jnp.arange(gd, dtype=jnp.int32)
    c, s = _cos_sin(jk[:, None] * jk[None, :], gd)
    chan = (jnp.concatenate([c, -s], axis=1) * gd ** -0.5).astype(BF16)

    n1, n2 = FFT_N1, seq // FFT_N1
    a = jnp.arange(n1, dtype=jnp.int32)
    c1, s1 = _cos_sin(a[:, None] * a[None, :], n1)
    m1 = (jnp.concatenate([jnp.concatenate([c1, s1], axis=1),
                           jnp.concatenate([-s1, c1], axis=1)], axis=0) * n1 ** -0.5).astype(BF16)

    k1 = jnp.arange(n1, dtype=jnp.int32)[:, None]
    k2 = jnp.arange(n2, dtype=jnp.int32)[:, None]
    nn = jnp.arange(n2, dtype=jnp.int32)[None, :]
    ca, sa = _cos_sin(k1 * nn, seq)
    cb, sb = _cos_sin(k2 * nn, n2)
    cd = ca[:, None, :] * cb[None] - sa[:, None, :] * sb[None]
    sd = sa[:, None, :] * cb[None] + ca[:, None, :] * sb[None]
    dmat = (jnp.concatenate([cd, sd], axis=2) * n2 ** -0.5).astype(BF16)
    return chan, m1, dmat


def _head_blocks(w):
    depth, rows, heads, used = w.shape
    pad = jnp.zeros((depth, rows, heads, HEAD_PAD - used), w.dtype)
    return jnp.concatenate([w, pad], axis=3).reshape(depth, rows, heads * HEAD_PAD).astype(BF16)


def _prep_weights(w_in, w_uq, w_ukv):
    depth, d = w_in.shape[:2]
    wt = jnp.swapaxes(w_in, 1, 2)
    qc = (wt[:, 544:1056] * ((GQA_HEAD_DIM ** -0.5) * LOG2E)).astype(BF16)
    wt = wt.astype(BF16)
    hd = GQA_HEAD_DIM
    dup = lambda w: jnp.concatenate([w[:, :hd], w[:, :hd], w[:, hd:], w[:, hd:]], axis=1)
    zrows = lambda n: jnp.zeros((depth, n, d), BF16)
    w_in_p = jnp.swapaxes(jnp.concatenate(
        [wt[:, :512], zrows(MLA_NOPE), wt[:, 512:544], zrows(HEAD_PAD - MLA_QK), qc,
         dup(wt[:, 1056:1184]), dup(wt[:, 1184:1312]), wt[:, 1312:]], axis=1), 1, 2)

    uq = w_uq.reshape(depth, MLA_Q_LORA, MLA_HEADS, MLA_QK) * ((MLA_QK ** -0.5) * LOG2E)
    wq = _head_blocks(uq)
    ukv = w_ukv.reshape(depth, MLA_KV_LORA, MLA_HEADS, MLA_NOPE + MLA_V)
    wk = _head_blocks(ukv[..., :MLA_NOPE])
    wv = _head_blocks(ukv[..., MLA_NOPE:])
    return w_in_p, wq, wk, wv


def kernel(x, p, positions, norm_mix_pre, w_in, mla_q_norm, w_uq, mla_kv_norm, w_ukv, gqa_sink,
           w_branch_a, w_branch_b, w_branch_c, w_out, norm_mix_post, norm_ffn_pre, w_ffn_gate,
           w_ffn_up, w_ffn_down, norm_ffn_post, w_ple_proj, w_ple_gate, norm_ple):
    b, s, d = x.shape
    depth = w_in.shape[0]
    t = b * s
    n1, n2 = FFT_N1, s // FFT_N1
    assert d == D_MODEL and s % TOKEN_TILE == 0 and s % FLASH_TQ == 0 and s % FLASH_TK == 0
    assert n2 % FFT_A_N2_BLOCK == 0 and d % FFT_A_COLS == 0 and (s // WIN_BLOCK) % GQA_NQ == 0

    chan, m1, dmat = _dft_tables(s)
    half = MLA_ROPE // 2
    inv_freq = ROPE_THETA ** (-jnp.arange(half, dtype=F32) / half)
    pos_dense = jnp.repeat(positions.reshape(t // ROPE_PACK, ROPE_PACK).astype(F32), half, axis=1)
    invf_dense = jnp.tile(inv_freq, ROPE_PACK)[None, :]
    src = jnp.arange(LANES, dtype=jnp.int32)[None, :, None]
    dst = jnp.arange(LANES, dtype=jnp.int32)[None, None, :]
    slot = jnp.arange(ROPE_PACK, dtype=jnp.int32)[:, None, None]
    place = ((src // half == slot) & (dst >= MLA_NOPE) & (dst < MLA_QK)
             & ((dst - MLA_NOPE) % half == src % half)).astype(BF16)
    pos_blocks = positions.reshape(b * (s // WIN_BLOCK), 1, WIN_BLOCK)
    vbias = jnp.zeros((MLA_HEADS, HEAD_PAD), F32).at[:, MLA_V].set(1.0).reshape(1, -1)
    slopes = tuple(float(v) for v in
                   2.0 ** (-8.0 * (np.arange(GQA_Q_HEADS, dtype=np.float32) + 1.0) / GQA_Q_HEADS))
    r1 = lambda v: v.reshape(1, -1)
    tail_w = [w.astype(BF16) for w in (w_branch_a, w_branch_b, w_branch_c, w_out, w_ffn_gate,
                                       w_ffn_up, w_ffn_down, w_ple_proj, w_ple_gate)]
    wa_all, wb_all, wc_all, wo_all, wg_all, wu_all, wd_all, wp_all, wpg_all = tail_w
    w_in_p, wq, wk, wv = _prep_weights(w_in, w_uq, w_ukv)

    rope_cos, rope_sin = _rope_table(pos_dense, invf_dense, place)

    x2 = x.reshape(t, d)
    for i in range(depth):
        qkv, gates, wr, wi, q, k, v = _inproj(
            x2, r1(norm_mix_pre[i]), w_in_p, chan, rope_cos, rope_sin,
            r1(mla_q_norm[i]), r1(mla_kv_norm[i]), wq, wk, wv, vbias, i)
        hw = MLA_HEADS * HEAD_PAD
        o_a = _flash(q.reshape(b, s, hw), k.reshape(b, s, hw), v.reshape(b, s, hw))

        ar, ai = _fft_a(wr.reshape(b, n1, n2, d), wi.reshape(b, n1, n2, d), m1)
        o_b = _fft_b(ar, ai, dmat)

        o_c = _gqa(qkv.reshape(b, s, -1), pos_blocks, gqa_sink[i], slopes)

        x2 = _token_tail(
            x2, o_a.reshape(t, -1), o_b.reshape(t, d), o_c.reshape(t, -1), gates,
            p, wa_all, wb_all, wc_all, wo_all,
            r1(norm_mix_post[i]), r1(norm_ffn_pre[i]), wg_all, wu_all,
            wd_all, r1(norm_ffn_post[i]), wp_all, wpg_all,
            r1(norm_ple[i]), i)
    return x2.reshape(b, s, d)
```

```python
import functools
import math

import numpy as np
import jax
import jax.numpy as jnp
from jax import lax
from jax.experimental import pallas as pl
from jax.experimental.pallas import tpu as pltpu

F32 = jnp.float32
BF16 = jnp.bfloat16

D_MODEL = 1024
PLE_DIM = 256
MLA_HEADS = 8
MLA_NOPE = 64
MLA_ROPE = 32
MLA_V = 64
MLA_QK = MLA_NOPE + MLA_ROPE
MLA_Q_LORA = 384
MLA_KV_LORA = 128
ROPE_THETA = 10000.0
FNET_GROUPS = 4
FNET_GROUP_DIM = D_MODEL // FNET_GROUPS
GQA_Q_HEADS = 8
GQA_KV_HEADS = 2
GQA_GROUP = GQA_Q_HEADS // GQA_KV_HEADS
GQA_HEAD_DIM = 64
WINDOW = 128
WIN_BLOCK = 128
FFN_DIM = 2816
RMS_EPS = 1e-6

LANES = 128
HEAD_PAD = 128
FFT_N1 = 128
VMEM_LIMIT = 56 * 1024 * 1024

TOKEN_TILE = 512
ROPE_PACK = LANES // (MLA_ROPE // 2)
FLASH_TQ = 1024
FLASH_TK = 1024
GQA_NQ = 8
FFT_A_N2_BLOCK = 16
FFT_A_COLS = 512
FFT_B_ROWS = 32
FFN_CHUNK = 256
LOG2E = 1.4426950408889634

SEG_CQ = (0, 384)
SEG_CKV = (384, 640)
SEG_QKV = (640, 1664)
SEG_GATE = (1664, 4736)
IN_COLS_PAD = 4736


def _params(n_grid_axes):
    return pltpu.CompilerParams(
        dimension_semantics=("arbitrary",) * n_grid_axes,
        vmem_limit_bytes=VMEM_LIMIT)


def _const_spec(shape):
    zeros = (0,) * len(shape)
    return pl.BlockSpec(shape, lambda *_: zeros, pipeline_mode=pl.Buffered(1))


def _layer_spec(shape, layer):
    zeros = (0,) * len(shape)
    return pl.BlockSpec((None,) + tuple(shape), lambda *_: (layer,) + zeros,
                        pipeline_mode=pl.Buffered(1))


def _rms(x, gain):
    ms = jnp.mean(x * x, axis=-1, keepdims=True)
    return x * lax.rsqrt(ms + RMS_EPS) * gain


def _dot(a, b):
    return jnp.dot(a, b, preferred_element_type=F32)


def _dot_nt(a, b):
    return lax.dot_general(a, b, (((1,), (1,)), ((), ())), preferred_element_type=F32)


def _inproj_kernel(x_ref, gain_ref, w_ref, dft_ref, cos_ref, sin_ref, qn_ref, kvn_ref,
                   wq_ref, wk_ref, wv_ref, vb_ref,
                   qkv_ref, gate_ref, wr_ref, wi_ref, q_ref, k_ref, v_ref):
    h = _rms(x_ref[...], gain_ref[...]).astype(BF16)
    cq = _dot(h, w_ref[:, SEG_CQ[0]:SEG_CQ[1]])
    ckv = _dot(h, w_ref[:, SEG_CKV[0]:SEG_CKV[1]])
    qkv_ref[...] = _dot(h, w_ref[:, SEG_QKV[0]:SEG_QKV[1]]).astype(BF16)
    for c in range(3):
        lo = SEG_GATE[0] + c * D_MODEL
        g = _dot(h, w_ref[:, lo:lo + D_MODEL])
        gate_ref[:, c * D_MODEL:(c + 1) * D_MODEL] = jax.nn.sigmoid(g).astype(BF16)

    cqn = _rms(cq, qn_ref[...]).astype(BF16)
    ckvn = _rms(ckv[:, :MLA_KV_LORA], kvn_ref[...]).astype(BF16)
    cos = cos_ref[...]
    sin = sin_ref[...]
    half = MLA_ROPE // 2
    lane = lax.broadcasted_iota(jnp.int32, cos.shape, 1)
    second = lane >= MLA_NOPE + half
    sin_up = jnp.where(second, sin, 0.0)
    sin_dn = jnp.where(second, 0.0, -sin)

    def rope(x):
        return (x * cos + pltpu.roll(x, half, 1) * sin_up
                + pltpu.roll(x, HEAD_PAD - half, 1) * sin_dn)

    kr = rope(ckv[:, MLA_KV_LORA:])
    for hd in range(MLA_HEADS):
        sl = slice(hd * HEAD_PAD, (hd + 1) * HEAD_PAD)
        q_ref[:, sl] = rope(_dot(cqn, wq_ref[:, sl])).astype(BF16)
        k_ref[:, sl] = (_dot(ckvn, wk_ref[:, sl]) + kr).astype(BF16)
        v_ref[:, sl] = (_dot(ckvn, wv_ref[:, sl]) + vb_ref[:, sl]).astype(BF16)

    gd = FNET_GROUP_DIM
    for g in range(FNET_GROUPS):
        yz = _dot(h[:, g * gd:(g + 1) * gd], dft_ref[...])
        wr_ref[:, g * gd:(g + 1) * gd] = yz[:, :gd].astype(BF16)
        wi_ref[:, g * gd:(g + 1) * gd] = yz[:, gd:].astype(BF16)


def _inproj(x2, gain, w_in_p, dft_c, cos, sin, qn, kvn, wq, wk, wv, vb, layer):
    t = x2.shape[0]
    tm = TOKEN_TILE
    hw = MLA_HEADS * HEAD_PAD
    row = lambda n: pl.BlockSpec((tm, n), lambda i: (i, 0))
    outs = [(t, SEG_QKV[1] - SEG_QKV[0]), (t, 3 * D_MODEL), (t, D_MODEL), (t, D_MODEL),
            (t, hw), (t, hw), (t, hw)]
    return pl.pallas_call(
        _inproj_kernel,
        grid=(t // tm,),
        in_specs=[row(D_MODEL), _const_spec((1, D_MODEL)),
                  _layer_spec((D_MODEL, IN_COLS_PAD), layer),
                  _const_spec((FNET_GROUP_DIM, 2 * FNET_GROUP_DIM)),
                  row(LANES), row(LANES),
                  _const_spec((1, MLA_Q_LORA)), _const_spec((1, MLA_KV_LORA)),
                  _layer_spec((MLA_Q_LORA, hw), layer), _layer_spec((MLA_KV_LORA, hw), layer),
                  _layer_spec((MLA_KV_LORA, hw), layer), _const_spec((1, hw))],
        out_specs=[row(n) for _, n in outs],
        out_shape=[jax.ShapeDtypeStruct(s, BF16) for s in outs],
        compiler_params=_params(1),
        name="inproj",
    )(x2, gain, w_in_p, dft_c, cos, sin, qn, kvn, wq, wk, wv, vb)


def _rope_table_kernel(pos_ref, invf_ref, place_ref, cos_ref, sin_ref):
    ang = pos_ref[...] * invf_ref[...]
    per_row = ROPE_PACK
    lane = lax.broadcasted_iota(jnp.int32, (pos_ref.shape[0] * per_row, LANES), 1)
    on_rope = (lane >= MLA_NOPE) & (lane < MLA_QK)

    def spread(dense):
        hi = dense.astype(BF16)
        lo = (dense - hi.astype(F32)).astype(BF16)
        parts = [_dot(hi, place_ref[u]) + _dot(lo, place_ref[u]) for u in range(per_row)]
        return jnp.swapaxes(jnp.stack(parts), 0, 1).reshape(lane.shape)

    cos_ref[...] = jnp.where(on_rope, spread(jnp.cos(ang)), 1.0)
    sin_ref[...] = spread(jnp.sin(ang))


def _rope_table(pos_dense, invf_dense, place):
    rows = pos_dense.shape[0]
    t = rows * ROPE_PACK
    tm = TOKEN_TILE
    return pl.pallas_call(
        _rope_table_kernel,
        grid=(t // tm,),
        in_specs=[pl.BlockSpec((tm // ROPE_PACK, LANES), lambda i: (i, 0)),
                  _const_spec((1, LANES)), _const_spec(place.shape)],
        out_specs=[pl.BlockSpec((tm, LANES), lambda i: (i, 0))] * 2,
        out_shape=[jax.ShapeDtypeStruct((t, LANES), F32)] * 2,
        compiler_params=_params(1),
        name="rope_table",
    )(pos_dense, invf_dense, place)


def _flash_kernel(q_ref, k_ref, v_ref, o_ref, *, tk, nk):
    tq = q_ref.shape[1]
    heads = [slice(h * HEAD_PAD, (h + 1) * HEAD_PAD) for h in range(2)]
    qs = [q_ref[0, :, sl] for sl in heads]

    def scores(i):
        return [_dot_nt(q, k_ref[0, i * tk:(i + 1) * tk, sl]) for q, sl in zip(qs, heads)]

    state = [(jnp.full((tq, 1), -jnp.inf, F32), jnp.zeros((tq, HEAD_PAD), F32))] * 2
    ss = scores(0)
    for i in range(nk):
        ss_next = scores(i + 1) if i + 1 < nk else None
        new = []
        for (m, acc), s, sl in zip(state, ss, heads):
            m_new = jnp.maximum(m, jnp.max(s, axis=-1, keepdims=True))
            p = jnp.exp2((s - m_new).astype(BF16))
            alpha = jnp.exp2(m - m_new)
            v = v_ref[0, i * tk:(i + 1) * tk, sl]
            new.append((m_new, alpha * acc + _dot(p, v)))
        state, ss = new, ss_next
    outs = [acc[:, :MLA_V] / acc[:, MLA_V:MLA_V + 1] for _, acc in state]
    o_ref[0] = jnp.concatenate(outs, axis=-1).astype(BF16)


def _flash(q, k, v):
    b, s, hw = q.shape
    tq, tk = FLASH_TQ, FLASH_TK
    pair = 2 * HEAD_PAD
    return pl.pallas_call(
        functools.partial(_flash_kernel, tk=tk, nk=s // tk),
        grid=(b, MLA_HEADS // 2, s // tq),
        in_specs=[pl.BlockSpec((1, tq, pair), lambda bi, hi, qi: (bi, qi, hi)),
                  pl.BlockSpec((1, s, pair), lambda bi, hi, qi: (bi, 0, hi)),
                  pl.BlockSpec((1, s, pair), lambda bi, hi, qi: (bi, 0, hi))],
        out_specs=pl.BlockSpec((1, tq, 2 * MLA_V), lambda bi, hi, qi: (bi, qi, hi)),
        out_shape=jax.ShapeDtypeStruct((b, s, MLA_HEADS * MLA_V), BF16),
        compiler_params=_params(3),
        name="mla_flash",
    )(q, k, v)


def _fft_a_kernel(xr_ref, xi_ref, m1_ref, ar_ref, ai_ref):
    xr = jnp.swapaxes(xr_ref[0], 0, 1)
    xi = jnp.swapaxes(xi_ref[0], 0, 1)
    out_r, out_i = [], []
    for j in range(xr.shape[0]):
        a = _dot(m1_ref[...], jnp.concatenate([xr[j], xi[j]], axis=0))
        out_r.append(a[:FFT_N1].astype(BF16))
        out_i.append(a[FFT_N1:].astype(BF16))
    ar_ref[0] = jnp.stack(out_r)
    ai_ref[0] = jnp.stack(out_i)


def _fft_a(wr, wi, m1):
    b, n1, n2, c = wr.shape
    nb2, tc = FFT_A_N2_BLOCK, FFT_A_COLS
    blk = pl.BlockSpec((1, n1, nb2, tc), lambda bi, ji, ci: (bi, 0, ji, ci))
    out_blk = pl.BlockSpec((1, nb2, n1, tc), lambda bi, ji, ci: (bi, ji, 0, ci))
    return pl.pallas_call(
        _fft_a_kernel,
        grid=(b, n2 // nb2, c // tc),
        in_specs=[blk, blk, _const_spec((2 * n1, 2 * n1))],
        out_specs=[out_blk, out_blk],
        out_shape=[jax.ShapeDtypeStruct((b, n2, n1, c), BF16)] * 2,
        compiler_params=_params(3),
        name="fft_a",
    )(wr, wi, m1)


def _fft_b_kernel(ar_ref, ai_ref, d_ref, o_ref, *, kb):
    ar = jnp.swapaxes(ar_ref[0], 0, 1)
    ai = jnp.swapaxes(ai_ref[0], 0, 1)
    res = []
    for j in range(kb):
        a = jnp.concatenate([ar[j], ai[j]], axis=0)
        res.append(_dot(d_ref[j], a).astype(BF16))
    o_ref[0] = jnp.swapaxes(jnp.stack(res), 0, 1)


def _fft_b(ar, ai, dmat):
    b, n2, n1, c = ar.shape
    kb = FFT_B_ROWS
    blk = pl.BlockSpec((1, n2, kb, c), lambda bi, ki: (bi, 0, ki, 0))
    return pl.pallas_call(
        functools.partial(_fft_b_kernel, kb=kb),
        grid=(b, n1 // kb),
        in_specs=[blk, blk, pl.BlockSpec((kb, n2, 2 * n2), lambda bi, ki: (ki, 0, 0))],
        out_specs=pl.BlockSpec((1, n2, kb, c), lambda bi, ki: (bi, 0, ki, 0)),
        out_shape=jax.ShapeDtypeStruct((b, n2, n1, c), BF16),
        compiler_params=_params(2),
        name="fft_b",
    )(ar, ai, dmat)


def _gqa_kernel(sink_ref, q_ref, kp_ref, kc_ref, kn_ref, vp_ref, vc_ref, vn_ref,
                pp_ref, pc_ref, pn_ref, o_ref, k_scr, v_scr, p_scr, *, seq, nq, slopes):
    j = pl.program_id(1)
    wb = WIN_BLOCK
    r = nq * wb
    k_scr[0:wb] = kp_ref[0]
    k_scr[wb:wb + r] = kc_ref[0]
    k_scr[wb + r:] = kn_ref[0]
    v_scr[0:wb] = vp_ref[0]
    v_scr[wb:wb + r] = vc_ref[0]
    v_scr[wb + r:] = vn_ref[0]
    p_scr[0] = pp_ref[0].astype(F32)
    p_scr[1:nq + 1] = pc_ref[...].astype(F32)
    p_scr[nq + 1] = pn_ref[0].astype(F32)

    kj = lax.broadcasted_iota(jnp.int32, (wb, 3 * wb), 1)
    qi = lax.broadcasted_iota(jnp.int32, (wb, 3 * wb), 0)
    band = jnp.abs(kj - wb - qi) <= WINDOW
    lane_kv = lax.broadcasted_iota(jnp.int32, (3 * wb, LANES), 1)
    low_half = lane_kv < GQA_HEAD_DIM
    ones_even = (lane_kv == GQA_HEAD_DIM).astype(BF16)
    ones_odd = (lane_kv == 0).astype(BF16)
    low_out = lax.broadcasted_iota(jnp.int32, (wb, LANES), 1) < GQA_HEAD_DIM

    def score_phase(a):
        rows = slice(a * wb, (a + 3) * wb)
        pc = p_scr[a + 1]
        pq = jnp.broadcast_to(pc, (wb, wb)).T
        dist = jnp.concatenate(
            [jnp.abs(pq - p_scr[a]), jnp.abs(pq - pc), jnp.abs(pq - p_scr[a + 2])], axis=1)
        key_idx = (j * nq + a) * wb - wb + kj
        mask = band & (key_idx >= 0) & (key_idx < seq)
        nd = jnp.where(mask, -dist, -jnp.inf)
        k_half = []
        for hk in range(GQA_KV_HEADS):
            kd = k_scr[rows, hk * LANES:(hk + 1) * LANES]
            zero = jnp.zeros_like(kd)
            k_half.append((jnp.where(low_half, kd, zero), jnp.where(low_half, zero, kd)))
        scores = []
        for hq in range(GQA_Q_HEADS):
            pair, par, hk = hq // 2, hq % 2, hq // GQA_GROUP
            qp = q_ref[0, a * wb:(a + 1) * wb, pair * LANES:(pair + 1) * LANES]
            scores.append(_dot_nt(qp, k_half[hk][par]) + (slopes[hq] * LOG2E) * nd)
        return scores

    def value_phase(a, scores):
        rows = slice(a * wb, (a + 3) * wb)
        v_half = []
        for hk in range(GQA_KV_HEADS):
            vd = v_scr[rows, hk * LANES:(hk + 1) * LANES]
            v_half.append((jnp.where(low_half, vd, ones_even), jnp.where(low_half, ones_odd, vd)))
        probs, sink_terms = [], []
        for hq, s in enumerate(scores):
            sink = sink_ref[hq] * LOG2E
            m = jnp.maximum(jnp.max(s, axis=-1, keepdims=True), sink)
            probs.append(jnp.exp2(s - m).astype(BF16))
            sink_terms.append(jnp.exp2(sink - m))
        outs = []
        for pair in range(GQA_Q_HEADS // 2):
            hk = (2 * pair) // GQA_GROUP
            even, odd = 2 * pair, 2 * pair + 1
            acc_e = _dot(probs[even], v_half[hk][0])
            acc_o = _dot(probs[odd], v_half[hk][1])
            r_e = 1.0 / (acc_e[:, GQA_HEAD_DIM:GQA_HEAD_DIM + 1] + sink_terms[even])
            r_o = 1.0 / (acc_o[:, 0:1] + sink_terms[odd])
            outs.append(jnp.where(low_out, acc_e * r_e, acc_o * r_o))
        o_ref[0, a * wb:(a + 1) * wb, :] = jnp.concatenate(outs, axis=-1).astype(BF16)

    scores = score_phase(0)
    for a in range(nq):
        scores_next = score_phase(a + 1) if a + 1 < nq else None
        value_phase(a, scores)
        scores = scores_next


def _gqa(qkv, pos_blocks, sink, slopes):
    b, s, _ = qkv.shape
    wb = WIN_BLOCK
    nb = s // wb
    nq = GQA_NQ
    r = nq * wb
    prev = lambda j: jnp.maximum(j * nq - 1, 0)
    nxt = lambda j: jnp.minimum((j + 1) * nq, nb - 1)
    edge = lambda lane_blk, f: pl.BlockSpec((1, wb, 2 * LANES), lambda bi, j: (bi, f(j), lane_blk))
    own = lambda lane_blk: pl.BlockSpec((1, r, 2 * LANES), lambda bi, j: (bi, j, lane_blk))
    pedge = lambda f: pl.BlockSpec((1, 1, wb), lambda bi, j: (bi * nb + f(j), 0, 0))
    qw = GQA_Q_HEADS * GQA_HEAD_DIM
    return pl.pallas_call(
        functools.partial(_gqa_kernel, seq=s, nq=nq, slopes=slopes),
        grid=(b, nb // nq),
        in_specs=[pl.BlockSpec(memory_space=pltpu.SMEM),
                  pl.BlockSpec((1, r, qw), lambda bi, j: (bi, j, 0)),
                  edge(2, prev), own(2), edge(2, nxt),
                  edge(3, prev), own(3), edge(3, nxt),
                  pedge(prev),
                  pl.BlockSpec((nq, 1, wb), lambda bi, j: (bi * (nb // nq) + j, 0, 0)),
                  pedge(nxt)],
        out_specs=pl.BlockSpec((1, r, qw), lambda bi, j: (bi, j, 0)),
        out_shape=jax.ShapeDtypeStruct((b, s, qw), BF16),
        scratch_shapes=[pltpu.VMEM((r + 2 * wb, 2 * LANES), BF16),
                        pltpu.VMEM((r + 2 * wb, 2 * LANES), BF16),
                        pltpu.VMEM((nq + 2, 1, wb), F32)],
        compiler_params=_params(2),
        name="gqa_window",
    )(sink, qkv, qkv, qkv, qkv, qkv, qkv, qkv, pos_blocks, pos_blocks, pos_blocks)


def _token_kernel(x_ref, oa_ref, ob_ref, oc_ref, gate_ref, p_ref,
                  wa_ref, wb_ref, wc_ref, wo_ref, g_mix_ref,
                  g_pre_ref, wg_ref, wu_ref, wd_ref, g_post_ref,
                  wp_ref, wpg_ref, g_ple_ref, out_ref):
    d = D_MODEL
    merged = (gate_ref[:, 0:d].astype(F32) * _dot(oa_ref[...], wa_ref[...])
              + gate_ref[:, d:2 * d].astype(F32) * _dot(ob_ref[...], wb_ref[...])
              + gate_ref[:, 2 * d:3 * d].astype(F32) * _dot(oc_ref[...], wc_ref[...]))
    x = x_ref[...] + _rms(_dot(merged.astype(BF16), wo_ref[...]), g_mix_ref[...])

    h = _rms(x, g_pre_ref[...]).astype(BF16)
    ff = jnp.zeros(x.shape, F32)
    for c in range(FFN_DIM // FFN_CHUNK):
        sl = slice(c * FFN_CHUNK, (c + 1) * FFN_CHUNK)
        a = jax.nn.silu(_dot(h, wg_ref[:, sl])) * _dot(h, wu_ref[:, sl])
        ff = ff + _dot(a.astype(BF16), wd_ref[sl, :])
    x = x + _rms(ff, g_post_ref[...])

    e = _dot(p_ref[...].astype(BF16), wp_ref[...]) * jax.nn.sigmoid(
        _dot(x.astype(BF16), wpg_ref[...]))
    out_ref[...] = x + _rms(e, g_ple_ref[...])


def _token_tail(x2, oa, ob, oc, gates, p2, wa, wb, wc, wo, g_mix,
                g_pre, wg, wu, wd, g_post, wp, wpg, g_ple, layer):
    t = x2.shape[0]
    tm = TOKEN_TILE
    d = D_MODEL
    row = lambda n: pl.BlockSpec((tm, n), lambda i: (i, 0))
    tiles_per_seq = p2.shape[2] // tm
    p_row = pl.BlockSpec((None, None, tm, PLE_DIM),
                         lambda i: (layer, i // tiles_per_seq, i % tiles_per_seq, 0))
    vec = _const_spec((1, d))
    w = lambda a: _layer_spec(a.shape[1:], layer)
    return pl.pallas_call(
        _token_kernel,
        grid=(t // tm,),
        in_specs=[row(d), row(oa.shape[1]), row(d), row(oc.shape[1]), row(3 * d), p_row,
                  w(wa), w(wb), w(wc), w(wo), vec,
                  vec, w(wg), w(wu), w(wd), vec,
                  w(wp), w(wpg), vec],
        out_specs=row(d),
        out_shape=jax.ShapeDtypeStruct((t, d), F32),
        compiler_params=_params(1),
        name="token_tail",
    )(x2, oa, ob, oc, gates, p2, wa, wb, wc, wo, g_mix, g_pre, wg, wu, wd, g_post, wp, wpg, g_ple)


def _cos_sin(num, den):
    ang = (num % den).astype(F32) * (2.0 * math.pi / den)
    return jnp.cos(ang), jnp.sin(ang)


def _dft_tables(seq):
    gd = FNET_GROUP_DIM
    jk = jnp.arange(gd, dtype=jnp.int32)
    c, s = _cos_sin(jk[:, None] * jk[None, :], gd)
    chan = (jnp.concatenate([c, -s], axis=1) * gd ** -0.5).astype(BF16)

    n1, n2 = FFT_N1, seq // FFT_N1
    a = jnp.arange(n1, dtype=jnp.int32)
    c1, s1 = _cos_sin(a[:, None] * a[None, :], n1)
    m1 = (jnp.concatenate([jnp.concatenate([c1, s1], axis=1),
                           jnp.concatenate([-s1, c1], axis=1)], axis=0) * n1 ** -0.5).astype(BF16)

    k1 = jnp.arange(n1, dtype=jnp.int32)[:, None]
    k2 = jnp.arange(n2, dtype=jnp.int32)[:, None]
    nn = jnp.arange(n2, dtype=jnp.int32)[None, :]
    ca, sa = _cos_sin(k1 * nn, seq)
    cb, sb = _cos_sin(k2 * nn, n2)
    cd = ca[:, None, :] * cb[None] - sa[:, None, :] * sb[None]
    sd = sa[:, None, :] * cb[None] + ca[:, None, :] * sb[None]
    dmat = (jnp.concatenate([cd, sd], axis=2) * n2 ** -0.5).astype(BF16)
    return chan, m1, dmat


def _head_blocks(w):
    depth, rows, heads, used = w.shape
    pad = jnp.zeros((depth, rows, heads, HEAD_PAD - used), w.dtype)
    return jnp.concatenate([w, pad], axis=3).reshape(depth, rows, heads * HEAD_PAD).astype(BF16)


def _prep_weights(w_in, w_uq, w_ukv):
    depth, d = w_in.shape[:2]
    qc = (w_in[..., 544:1056] * ((GQA_HEAD_DIM ** -0.5) * LOG2E)).astype(BF16)
    wi = w_in.astype(BF16)
    hd = GQA_HEAD_DIM
    dup = lambda w: jnp.concatenate([w[..., :hd], w[..., :hd], w[..., hd:], w[..., hd:]], axis=-1)
    zcols = lambda n: jnp.zeros((depth, d, n), BF16)
    w_in_p = jnp.concatenate(
        [wi[..., :512], zcols(MLA_NOPE), wi[..., 512:544], zcols(HEAD_PAD - MLA_QK), qc,
         dup(wi[..., 1056:1184]), dup(wi[..., 1184:1312]), wi[..., 1312:]], axis=-1)

    uq = w_uq.reshape(depth, MLA_Q_LORA, MLA_HEADS, MLA_QK) * ((MLA_QK ** -0.5) * LOG2E)
    wq = _head_blocks(uq)
    ukv = w_ukv.reshape(depth, MLA_KV_LORA, MLA_HEADS, MLA_NOPE + MLA_V)
    wk = _head_blocks(ukv[..., :MLA_NOPE])
    wv = _head_blocks(ukv[..., MLA_NOPE:])
    return w_in_p, wq, wk, wv


def kernel(x, p, positions, norm_mix_pre, w_in, mla_q_norm, w_uq, mla_kv_norm, w_ukv, gqa_sink,
           w_branch_a, w_branch_b, w_branch_c, w_out, norm_mix_post, norm_ffn_pre, w_ffn_gate,
           w_ffn_up, w_ffn_down, norm_ffn_post, w_ple_proj, w_ple_gate, norm_ple):
    b, s, d = x.shape
    depth = w_in.shape[0]
    t = b * s
    n1, n2 = FFT_N1, s // FFT_N1
    assert d == D_MODEL and s % TOKEN_TILE == 0 and s % FLASH_TQ == 0 and s % FLASH_TK == 0
    assert n2 % FFT_A_N2_BLOCK == 0 and d % FFT_A_COLS == 0 and (s // WIN_BLOCK) % GQA_NQ == 0

    chan, m1, dmat = _dft_tables(s)
    half = MLA_ROPE // 2
    inv_freq = ROPE_THETA ** (-jnp.arange(half, dtype=F32) / half)
    pos_dense = jnp.repeat(positions.reshape(t // ROPE_PACK, ROPE_PACK).astype(F32), half, axis=1)
    invf_dense = jnp.tile(inv_freq, ROPE_PACK)[None, :]
    src = jnp.arange(LANES, dtype=jnp.int32)[None, :, None]
    dst = jnp.arange(LANES, dtype=jnp.int32)[None, None, :]
    slot = jnp.arange(ROPE_PACK, dtype=jnp.int32)[:, None, None]
    place = ((src // half == slot) & (dst >= MLA_NOPE) & (dst < MLA_QK)
             & ((dst - MLA_NOPE) % half == src % half)).astype(BF16)
    pos_blocks = positions.reshape(b * (s // WIN_BLOCK), 1, WIN_BLOCK)
    vbias = jnp.zeros((MLA_HEADS, HEAD_PAD), F32).at[:, MLA_V].set(1.0).reshape(1, -1)
    slopes = tuple(float(v) for v in
                   2.0 ** (-8.0 * (np.arange(GQA_Q_HEADS, dtype=np.float32) + 1.0) / GQA_Q_HEADS))
    r1 = lambda v: v.reshape(1, -1)
    tail_w = [w.astype(BF16) for w in (w_branch_a, w_branch_b, w_branch_c, w_out, w_ffn_gate,
                                       w_ffn_up, w_ffn_down, w_ple_proj, w_ple_gate)]
    wa_all, wb_all, wc_all, wo_all, wg_all, wu_all, wd_all, wp_all, wpg_all = tail_w
    w_in_p, wq, wk, wv = _prep_weights(w_in, w_uq, w_ukv)

    rope_cos, rope_sin = _rope_table(pos_dense, invf_dense, place)

    x2 = x.reshape(t, d)
    for i in range(depth):
        qkv, gates, wr, wi, q, k, v = _inproj(
            x2, r1(norm_mix_pre[i]), w_in_p, chan, rope_cos, rope_sin,
            r1(mla_q_norm[i]), r1(mla_kv_norm[i]), wq, wk, wv, vbias, i)
        hw = MLA_HEADS * HEAD_PAD
        o_a = _flash(q.reshape(b, s, hw), k.reshape(b, s, hw), v.reshape(b, s, hw))

        ar, ai = _fft_a(wr.reshape(b, n1, n2, d), wi.reshape(b, n1, n2, d), m1)
        o_b = _fft_b(ar, ai, dmat)

        o_c = _gqa(qkv.reshape(b, s, -1), pos_blocks, gqa_sink[i], slopes)

        x2 = _token_tail(
            x2, o_a.reshape(t, -1), o_b.reshape(t, d), o_c.reshape(t, -1), gates,
            p, wa_all, wb_all, wc_all, wo_all,
            r1(norm_mix_post[i]), r1(norm_ffn_pre[i]), wg_all, wu_all,
            wd_all, r1(norm_ffn_post[i]), wp_all, wpg_all,
            r1(norm_ple[i]), i)
    return x2.reshape(b, s, d)
```

```python
import functools
import math

import numpy as np
import jax
import jax.numpy as jnp
from jax import lax
from jax.experimental import pallas as pl
from jax.experimental.pallas import tpu as pltpu

F32 = jnp.float32
BF16 = jnp.bfloat16

D_MODEL = 1024
PLE_DIM = 256
MLA_HEADS = 8
MLA_NOPE = 64
MLA_ROPE = 32
MLA_V = 64
MLA_QK = MLA_NOPE + MLA_ROPE
MLA_Q_LORA = 384
MLA_KV_LORA = 128
ROPE_THETA = 10000.0
FNET_GROUPS = 4
FNET_GROUP_DIM = D_MODEL // FNET_GROUPS
GQA_Q_HEADS = 8
GQA_KV_HEADS = 2
GQA_GROUP = GQA_Q_HEADS // GQA_KV_HEADS
GQA_HEAD_DIM = 64
WINDOW = 128
WIN_BLOCK = 128
FFN_DIM = 2816
RMS_EPS = 1e-6

LANES = 128
HEAD_PAD = 128
FFT_N1 = 128
VMEM_LIMIT = 56 * 1024 * 1024

TOKEN_TILE = 512
ROPE_PACK = LANES // (MLA_ROPE // 2)
FLASH_TQ = 1024
FLASH_TK = 1024
GQA_NQ = 8
FFT_A_N2_BLOCK = 16
FFT_A_COLS = 1024
FFT_B_ROWS = 32
FFN_CHUNK = 256
LOG2E = 1.4426950408889634

SEG_CQ = (0, 384)
SEG_CKV = (384, 640)
SEG_QKV = (640, 1664)
SEG_GATE = (1664, 4736)
IN_COLS_PAD = 4736


def _params(n_grid_axes):
    return pltpu.CompilerParams(
        dimension_semantics=("arbitrary",) * n_grid_axes,
        vmem_limit_bytes=VMEM_LIMIT)


def _const_spec(shape):
    zeros = (0,) * len(shape)
    return pl.BlockSpec(shape, lambda *_: zeros, pipeline_mode=pl.Buffered(1))


def _layer_spec(shape, layer):
    zeros = (0,) * len(shape)
    return pl.BlockSpec((None,) + tuple(shape), lambda *_: (layer,) + zeros,
                        pipeline_mode=pl.Buffered(1))


def _rms(x, gain):
    ms = jnp.mean(x * x, axis=-1, keepdims=True)
    return x * lax.rsqrt(ms + RMS_EPS) * gain


def _dot(a, b):
    return jnp.dot(a, b, preferred_element_type=F32)


def _dot_nt(a, b):
    return lax.dot_general(a, b, (((1,), (1,)), ((), ())), preferred_element_type=F32)


def _inproj_kernel(x_ref, gain_ref, w_ref, dft_ref, cos_ref, sin_ref, qn_ref, kvn_ref,
                   wq_ref, wk_ref, wv_ref, vb_ref,
                   qkv_ref, gate_ref, wr_ref, wi_ref, q_ref, k_ref, v_ref):
    h = _rms(x_ref[...], gain_ref[...]).astype(BF16)
    cq = _dot(h, w_ref[:, SEG_CQ[0]:SEG_CQ[1]])
    ckv = _dot(h, w_ref[:, SEG_CKV[0]:SEG_CKV[1]])
    qkv_ref[...] = _dot(h, w_ref[:, SEG_QKV[0]:SEG_QKV[1]]).astype(BF16)
    for c in range(3):
        lo = SEG_GATE[0] + c * D_MODEL
        g = _dot(h, w_ref[:, lo:lo + D_MODEL])
        gate_ref[:, c * D_MODEL:(c + 1) * D_MODEL] = jax.nn.sigmoid(g).astype(BF16)

    cqn = _rms(cq, qn_ref[...]).astype(BF16)
    ckvn = _rms(ckv[:, :MLA_KV_LORA], kvn_ref[...]).astype(BF16)
    cos = cos_ref[...]
    sin = sin_ref[...]
    half = MLA_ROPE // 2
    lane = lax.broadcasted_iota(jnp.int32, cos.shape, 1)
    second = lane >= MLA_NOPE + half
    sin_up = jnp.where(second, sin, 0.0)
    sin_dn = jnp.where(second, 0.0, -sin)

    def rope(x):
        return (x * cos + pltpu.roll(x, half, 1) * sin_up
                + pltpu.roll(x, HEAD_PAD - half, 1) * sin_dn)

    kr = rope(ckv[:, MLA_KV_LORA:])
    for hd in range(MLA_HEADS):
        sl = slice(hd * HEAD_PAD, (hd + 1) * HEAD_PAD)
        q_ref[:, sl] = rope(_dot(cqn, wq_ref[:, sl])).astype(BF16)
        k_ref[:, sl] = (_dot(ckvn, wk_ref[:, sl]) + kr).astype(BF16)
        v_ref[:, sl] = (_dot(ckvn, wv_ref[:, sl]) + vb_ref[:, sl]).astype(BF16)

    gd = FNET_GROUP_DIM
    for g in range(FNET_GROUPS):
        yz = _dot(h[:, g * gd:(g + 1) * gd], dft_ref[...])
        wr_ref[:, g * gd:(g + 1) * gd] = yz[:, :gd].astype(BF16)
        wi_ref[:, g * gd:(g + 1) * gd] = yz[:, gd:].astype(BF16)


def _inproj(x2, gain, w_in_p, dft_c, cos, sin, qn, kvn, wq, wk, wv, vb, layer):
    t = x2.shape[0]
    tm = TOKEN_TILE
    hw = MLA_HEADS * HEAD_PAD
    row = lambda n: pl.BlockSpec((tm, n), lambda i: (i, 0))
    outs = [(t, SEG_QKV[1] - SEG_QKV[0]), (t, 3 * D_MODEL), (t, D_MODEL), (t, D_MODEL),
            (t, hw), (t, hw), (t, hw)]
    return pl.pallas_call(
        _inproj_kernel,
        grid=(t // tm,),
        in_specs=[row(D_MODEL), _const_spec((1, D_MODEL)),
                  _layer_spec((D_MODEL, IN_COLS_PAD), layer),
                  _const_spec((FNET_GROUP_DIM, 2 * FNET_GROUP_DIM)),
                  row(LANES), row(LANES),
                  _const_spec((1, MLA_Q_LORA)), _const_spec((1, MLA_KV_LORA)),
                  _layer_spec((MLA_Q_LORA, hw), layer), _layer_spec((MLA_KV_LORA, hw), layer),
                  _layer_spec((MLA_KV_LORA, hw), layer), _const_spec((1, hw))],
        out_specs=[row(n) for _, n in outs],
        out_shape=[jax.ShapeDtypeStruct(s, BF16) for s in outs],
        compiler_params=_params(1),
        name="inproj",
    )(x2, gain, w_in_p, dft_c, cos, sin, qn, kvn, wq, wk, wv, vb)


def _rope_table_kernel(pos_ref, invf_ref, place_ref, cos_ref, sin_ref):
    ang = pos_ref[...] * invf_ref[...]
    per_row = ROPE_PACK
    lane = lax.broadcasted_iota(jnp.int32, (pos_ref.shape[0] * per_row, LANES), 1)
    on_rope = (lane >= MLA_NOPE) & (lane < MLA_QK)

    def spread(dense):
        hi = dense.astype(BF16)
        lo = (dense - hi.astype(F32)).astype(BF16)
        parts = [_dot(hi, place_ref[u]) + _dot(lo, place_ref[u]) for u in range(per_row)]
        return jnp.swapaxes(jnp.stack(parts), 0, 1).reshape(lane.shape)

    cos_ref[...] = jnp.where(on_rope, spread(jnp.cos(ang)), 1.0)
    sin_ref[...] = spread(jnp.sin(ang))


def _rope_table(pos_dense, invf_dense, place):
    rows = pos_dense.shape[0]
    t = rows * ROPE_PACK
    tm = TOKEN_TILE
    return pl.pallas_call(
        _rope_table_kernel,
        grid=(t // tm,),
        in_specs=[pl.BlockSpec((tm // ROPE_PACK, LANES), lambda i: (i, 0)),
                  _const_spec((1, LANES)), _const_spec(place.shape)],
        out_specs=[pl.BlockSpec((tm, LANES), lambda i: (i, 0))] * 2,
        out_shape=[jax.ShapeDtypeStruct((t, LANES), F32)] * 2,
        compiler_params=_params(1),
        name="rope_table",
    )(pos_dense, invf_dense, place)


def _flash_kernel(q_ref, k_ref, v_ref, o_ref, *, tk, nk):
    tq = q_ref.shape[1]
    heads = [slice(h * HEAD_PAD, (h + 1) * HEAD_PAD) for h in range(2)]
    qs = [q_ref[0, :, sl] for sl in heads]

    def scores(i):
        return [_dot_nt(q, k_ref[0, i * tk:(i + 1) * tk, sl]) for q, sl in zip(qs, heads)]

    state = [(jnp.full((tq, 1), -jnp.inf, F32), jnp.zeros((tq, HEAD_PAD), F32))] * 2
    ss = scores(0)
    for i in range(nk):
        ss_next = scores(i + 1) if i + 1 < nk else None
        new = []
        for (m, acc), s, sl in zip(state, ss, heads):
            m_new = jnp.maximum(m, jnp.max(s, axis=-1, keepdims=True))
            p = jnp.exp2((s - m_new).astype(BF16))
            alpha = jnp.exp2(m - m_new)
            v = v_ref[0, i * tk:(i + 1) * tk, sl]
            new.append((m_new, alpha * acc + _dot(p, v)))
        state, ss = new, ss_next
    outs = [acc[:, :MLA_V] / acc[:, MLA_V:MLA_V + 1] for _, acc in state]
    o_ref[0] = jnp.concatenate(outs, axis=-1).astype(BF16)


def _flash(q, k, v):
    b, s, hw = q.shape
    tq, tk = FLASH_TQ, FLASH_TK
    pair = 2 * HEAD_PAD
    return pl.pallas_call(
        functools.partial(_flash_kernel, tk=tk, nk=s // tk),
        grid=(b, MLA_HEADS // 2, s // tq),
        in_specs=[pl.BlockSpec((1, tq, pair), lambda bi, hi, qi: (bi, qi, hi)),
                  pl.BlockSpec((1, s, pair), lambda bi, hi, qi: (bi, 0, hi)),
                  pl.BlockSpec((1, s, pair), lambda bi, hi, qi: (bi, 0, hi))],
        out_specs=pl.BlockSpec((1, tq, 2 * MLA_V), lambda bi, hi, qi: (bi, qi, hi)),
        out_shape=jax.ShapeDtypeStruct((b, s, MLA_HEADS * MLA_V), BF16),
        compiler_params=_params(3),
        name="mla_flash",
    )(q, k, v)


def _fft_a_kernel(xr_ref, xi_ref, m1_ref, ar_ref, ai_ref):
    xr = jnp.swapaxes(xr_ref[0], 0, 1)
    xi = jnp.swapaxes(xi_ref[0], 0, 1)
    out_r, out_i = [], []
    for j in range(xr.shape[0]):
        a = _dot(m1_ref[...], jnp.concatenate([xr[j], xi[j]], axis=0))
        out_r.append(a[:FFT_N1].astype(BF16))
        out_i.append(a[FFT_N1:].astype(BF16))
    ar_ref[0] = jnp.stack(out_r)
    ai_ref[0] = jnp.stack(out_i)


def _fft_a(wr, wi, m1):
    b, n1, n2, c = wr.shape
    nb2, tc = FFT_A_N2_BLOCK, FFT_A_COLS
    blk = pl.BlockSpec((1, n1, nb2, tc), lambda bi, ji, ci: (bi, 0, ji, ci))
    out_blk = pl.BlockSpec((1, nb2, n1, tc), lambda bi, ji, ci: (bi, ji, 0, ci))
    return pl.pallas_call(
        _fft_a_kernel,
        grid=(b, n2 // nb2, c // tc),
        in_specs=[blk, blk, _const_spec((2 * n1, 2 * n1))],
        out_specs=[out_blk, out_blk],
        out_shape=[jax.ShapeDtypeStruct((b, n2, n1, c), BF16)] * 2,
        compiler_params=_params(3),
        name="fft_a",
    )(wr, wi, m1)


def _fft_b_kernel(ar_ref, ai_ref, d_ref, o_ref, *, kb):
    ar = jnp.swapaxes(ar_ref[0], 0, 1)
    ai = jnp.swapaxes(ai_ref[0], 0, 1)
    res = []
    for j in range(kb):
        a = jnp.concatenate([ar[j], ai[j]], axis=0)
        res.append(_dot(d_ref[j], a).astype(BF16))
    o_ref[0] = jnp.swapaxes(jnp.stack(res), 0, 1)


def _fft_b(ar, ai, dmat):
    b, n2, n1, c = ar.shape
    kb = FFT_B_ROWS
    blk = pl.BlockSpec((1, n2, kb, c), lambda bi, ki: (bi, 0, ki, 0))
    return pl.pallas_call(
        functools.partial(_fft_b_kernel, kb=kb),
        grid=(b, n1 // kb),
        in_specs=[blk, blk, pl.BlockSpec((kb, n2, 2 * n2), lambda bi, ki: (ki, 0, 0))],
        out_specs=pl.BlockSpec((1, n2, kb, c), lambda bi, ki: (bi, 0, ki, 0)),
        out_shape=jax.ShapeDtypeStruct((b, n2, n1, c), BF16),
        compiler_params=_params(2),
        name="fft_b",
    )(ar, ai, dmat)


def _gqa_kernel(sink_ref, q_ref, kp_ref, kc_ref, kn_ref, vp_ref, vc_ref, vn_ref,
                pp_ref, pc_ref, pn_ref, o_ref, k_scr, v_scr, p_scr, *, seq, nq, slopes):
    j = pl.program_id(1)
    wb = WIN_BLOCK
    r = nq * wb
    k_scr[0:wb] = kp_ref[0]
    k_scr[wb:wb + r] = kc_ref[0]
    k_scr[wb + r:] = kn_ref[0]
    v_scr[0:wb] = vp_ref[0]
    v_scr[wb:wb + r] = vc_ref[0]
    v_scr[wb + r:] = vn_ref[0]
    p_scr[0] = pp_ref[0].astype(F32)
    p_scr[1:nq + 1] = pc_ref[...].astype(F32)
    p_scr[nq + 1] = pn_ref[0].astype(F32)

    kj = lax.broadcasted_iota(jnp.int32, (wb, 3 * wb), 1)
    qi = lax.broadcasted_iota(jnp.int32, (wb, 3 * wb), 0)
    band = jnp.abs(kj - wb - qi) <= WINDOW
    lane_kv = lax.broadcasted_iota(jnp.int32, (3 * wb, LANES), 1)
    low_half = lane_kv < GQA_HEAD_DIM
    ones_even = (lane_kv == GQA_HEAD_DIM).astype(BF16)
    ones_odd = (lane_kv == 0).astype(BF16)
    low_out = lax.broadcasted_iota(jnp.int32, (wb, LANES), 1) < GQA_HEAD_DIM

    def score_phase(a):
        rows = slice(a * wb, (a + 3) * wb)
        pc = p_scr[a + 1]
        pq = jnp.broadcast_to(pc, (wb, wb)).T
        dist = jnp.concatenate(
            [jnp.abs(pq - p_scr[a]), jnp.abs(pq - pc), jnp.abs(pq - p_scr[a + 2])], axis=1)
        key_idx = (j * nq + a) * wb - wb + kj
        mask = band & (key_idx >= 0) & (key_idx < seq)
        nd = jnp.where(mask, -dist, -jnp.inf)
        k_half = []
        for hk in range(GQA_KV_HEADS):
            kd = k_scr[rows, hk * LANES:(hk + 1) * LANES]
            zero = jnp.zeros_like(kd)
            k_half.append((jnp.where(low_half, kd, zero), jnp.where(low_half, zero, kd)))
        scores = []
        for hq in range(GQA_Q_HEADS):
            pair, par, hk = hq // 2, hq % 2, hq // GQA_GROUP
            qp = q_ref[0, a * wb:(a + 1) * wb, pair * LANES:(pair + 1) * LANES]
            scores.append(_dot_nt(qp, k_half[hk][par]) + (slopes[hq] * LOG2E) * nd)
        return scores

    def value_phase(a, scores):
        rows = slice(a * wb, (a + 3) * wb)
        v_half = []
        for hk in range(GQA_KV_HEADS):
            vd = v_scr[rows, hk * LANES:(hk + 1) * LANES]
            v_half.append((jnp.where(low_half, vd, ones_even), jnp.where(low_half, ones_odd, vd)))
        probs, sink_terms = [], []
        for hq, s in enumerate(scores):
            sink = sink_ref[hq] * LOG2E
            m = jnp.maximum(jnp.max(s, axis=-1, keepdims=True), sink)
            probs.append(jnp.exp2((s - m).astype(BF16)))
            sink_terms.append(jnp.exp2(sink - m))
        outs = []
        for pair in range(GQA_Q_HEADS // 2):
            hk = (2 * pair) // GQA_GROUP
            even, odd = 2 * pair, 2 * pair + 1
            acc_e = _dot(probs[even], v_half[hk][0])
            acc_o = _dot(probs[odd], v_half[hk][1])
            r_e = 1.0 / (acc_e[:, GQA_HEAD_DIM:GQA_HEAD_DIM + 1] + sink_terms[even])
            r_o = 1.0 / (acc_o[:, 0:1] + sink_terms[odd])
            outs.append(jnp.where(low_out, acc_e * r_e, acc_o * r_o))
        o_ref[0, a * wb:(a + 1) * wb, :] = jnp.concatenate(outs, axis=-1).astype(BF16)

    scores = score_phase(0)
    for a in range(nq):
        scores_next = score_phase(a + 1) if a + 1 < nq else None
        value_phase(a, scores)
        scores = scores_next


def _gqa(qkv, pos_blocks, sink, slopes):
    b, s, _ = qkv.shape
    wb = WIN_BLOCK
    nb = s // wb
    nq = GQA_NQ
    r = nq * wb
    prev = lambda j: jnp.maximum(j * nq - 1, 0)
    nxt = lambda j: jnp.minimum((j + 1) * nq, nb - 1)
    edge = lambda lane_blk, f: pl.BlockSpec((1, wb, 2 * LANES), lambda bi, j: (bi, f(j), lane_blk))
    own = lambda lane_blk: pl.BlockSpec((1, r, 2 * LANES), lambda bi, j: (bi, j, lane_blk))
    pedge = lambda f: pl.BlockSpec((1, 1, wb), lambda bi, j: (bi * nb + f(j), 0, 0))
    qw = GQA_Q_HEADS * GQA_HEAD_DIM
    return pl.pallas_call(
        functools.partial(_gqa_kernel, seq=s, nq=nq, slopes=slopes),
        grid=(b, nb // nq),
        in_specs=[pl.BlockSpec(memory_space=pltpu.SMEM),
                  pl.BlockSpec((1, r, qw), lambda bi, j: (bi, j, 0)),
                  edge(2, prev), own(2), edge(2, nxt),
                  edge(3, prev), own(3), edge(3, nxt),
                  pedge(prev),
                  pl.BlockSpec((nq, 1, wb), lambda bi, j: (bi * (nb // nq) + j, 0, 0)),
                  pedge(nxt)],
        out_specs=pl.BlockSpec((1, r, qw), lambda bi, j: (bi, j, 0)),
        out_shape=jax.ShapeDtypeStruct((b, s, qw), BF16),
        scratch_shapes=[pltpu.VMEM((r + 2 * wb, 2 * LANES), BF16),
                        pltpu.VMEM((r + 2 * wb, 2 * LANES), BF16),
                        pltpu.VMEM((nq + 2, 1, wb), F32)],
        compiler_params=_params(2),
        name="gqa_window",
    )(sink, qkv, qkv, qkv, qkv, qkv, qkv, qkv, pos_blocks, pos_blocks, pos_blocks)


def _token_kernel(x_ref, oa_ref, ob_ref, oc_ref, gate_ref, p_ref,
                  wa_ref, wb_ref, wc_ref, wo_ref, g_mix_ref,
                  g_pre_ref, wg_ref, wu_ref, wd_ref, g_post_ref,
                  wp_ref, wpg_ref, g_ple_ref, out_ref):
    d = D_MODEL
    merged = (gate_ref[:, 0:d].astype(F32) * _dot(oa_ref[...], wa_ref[...])
              + gate_ref[:, d:2 * d].astype(F32) * _dot(ob_ref[...], wb_ref[...])
              + gate_ref[:, 2 * d:3 * d].astype(F32) * _dot(oc_ref[...], wc_ref[...]))
    x = x_ref[...] + _rms(_dot(merged.astype(BF16), wo_ref[...]), g_mix_ref[...])

    h = _rms(x, g_pre_ref[...]).astype(BF16)
    ff = jnp.zeros(x.shape, F32)
    for c in range(FFN_DIM // FFN_CHUNK):
        sl = slice(c * FFN_CHUNK, (c + 1) * FFN_CHUNK)
        a = jax.nn.silu(_dot(h, wg_ref[:, sl])) * _dot(h, wu_ref[:, sl])
        ff = ff + _dot(a.astype(BF16), wd_ref[sl, :])
    x = x + _rms(ff, g_post_ref[...])

    e = _dot(p_ref[...].astype(BF16), wp_ref[...]) * jax.nn.sigmoid(
        _dot(x.astype(BF16), wpg_ref[...]))
    out_ref[...] = x + _rms(e, g_ple_ref[...])


def _token_tail(x2, oa, ob, oc, gates, p2, wa, wb, wc, wo, g_mix,
                g_pre, wg, wu, wd, g_post, wp, wpg, g_ple, layer):
    t = x2.shape[0]
    tm = TOKEN_TILE
    d = D_MODEL
    row = lambda n: pl.BlockSpec((tm, n), lambda i: (i, 0))
    tiles_per_seq = p2.shape[2] // tm
    p_row = pl.BlockSpec((None, None, tm, PLE_DIM),
                         lambda i: (layer, i // tiles_per_seq, i % tiles_per_seq, 0))
    vec = _const_spec((1, d))
    w = lambda a: _layer_spec(a.shape[1:], layer)
    return pl.pallas_call(
        _token_kernel,
        grid=(t // tm,),
        in_specs=[row(d), row(oa.shape[1]), row(d), row(oc.shape[1]), row(3 * d), p_row,
                  w(wa), w(wb), w(wc), w(wo), vec,
                  vec, w(wg), w(wu), w(wd), vec,
                  w(wp), w(wpg), vec],
        out_specs=row(d),
        out_shape=jax.ShapeDtypeStruct((t, d), F32),
        compiler_params=_params(1),
        name="token_tail",
    )(x2, oa, ob, oc, gates, p2, wa, wb, wc, wo, g_mix, g_pre, wg, wu, wd, g_post, wp, wpg, g_ple)


def _cos_sin(num, den):
    ang = (num % den).astype(F32) * (2.0 * math.pi / den)
    return jnp.cos(ang), jnp.sin(ang)


def _dft_tables(seq):
    gd = FNET_GROUP_DIM
    jk = jnp.arange(gd, dtype=jnp.int32)
    c, s = _cos_sin(jk[:, None] * jk[None, :], gd)
    chan = (jnp.concatenate([c, -s], axis=1) * gd ** -0.5).astype(BF16)

    n1, n2 = FFT_N1, seq // FFT_N1
    a = jnp.arange(n1, dtype=jnp.int32)
    c1, s1 = _cos_sin(a[:, None] * a[None, :], n1)
    m1 = (jnp.concatenate([jnp.concatenate([c1, s1], axis=1),
                           jnp.concatenate([-s1, c1], axis=1)], axis=0) * n1 ** -0.5).astype(BF16)

    k1 = jnp.arange(n1, dtype=jnp.int32)[:, None]
    k2 = jnp.arange(n2, dtype=jnp.int32)[:, None]
    nn = jnp.arange(n2, dtype=jnp.int32)[None, :]
    ca, sa = _cos_sin(k1 * nn, seq)
    cb, sb = _cos_sin(k2 * nn, n2)
    cd = ca[:, None, :] * cb[None] - sa[:, None, :] * sb[None]
    sd = sa[:, None, :] * cb[None] + ca[:, None, :] * sb[None]
    dmat = (jnp.concatenate([cd, sd], axis=2) * n2 ** -0.5).astype(BF16)
    return chan, m1, dmat


def _head_blocks(w):
    depth, rows, heads, used = w.shape
    pad = jnp.zeros((depth, rows, heads, HEAD_PAD - used), w.dtype)
    return jnp.concatenate([w, pad], axis=3).reshape(depth, rows, heads * HEAD_PAD).astype(BF16)


def _prep_weights(w_in, w_uq, w_ukv):
    depth, d = w_in.shape[:2]
    qc = (w_in[..., 544:1056] * ((GQA_HEAD_DIM ** -0.5) * LOG2E)).astype(BF16)
    wi = w_in.astype(BF16)
    hd = GQA_HEAD_DIM
    dup = lambda w: jnp.concatenate([w[..., :hd], w[..., :hd], w[..., hd:], w[..., hd:]], axis=-1)
    zcols = lambda n: jnp.zeros((depth, d, n), BF16)
    w_in_p = jnp.concatenate(
        [wi[..., :512], zcols(MLA_NOPE), wi[..., 512:544], zcols(HEAD_PAD - MLA_QK), qc,
         dup(wi[..., 1056:1184]), dup(wi[..., 1184:1312]), wi[..., 1312:]], axis=-1)

    uq = w_uq.reshape(depth, MLA_Q_LORA, MLA_HEADS, MLA_QK) * ((MLA_QK ** -0.5) * LOG2E)
    wq = _head_blocks(uq)
    ukv = w_ukv.reshape(depth, MLA_KV_LORA, MLA_HEADS, MLA_NOPE + MLA_V)
    wk = _head_blocks(ukv[..., :MLA_NOPE])
    wv = _head_blocks(ukv[..., MLA_NOPE:])
    return w_in_p, wq, wk, wv


def kernel(x, p, positions, norm_mix_pre, w_in, mla_q_norm, w_uq, mla_kv_norm, w_ukv, gqa_sink,
           w_branch_a, w_branch_b, w_branch_c, w_out, norm_mix_post, norm_ffn_pre, w_ffn_gate,
           w_ffn_up, w_ffn_down, norm_ffn_post, w_ple_proj, w_ple_gate, norm_ple):
    b, s, d = x.shape
    depth = w_in.shape[0]
    t = b * s
    n1, n2 = FFT_N1, s // FFT_N1
    assert d == D_MODEL and s % TOKEN_TILE == 0 and s % FLASH_TQ == 0 and s % FLASH_TK == 0
    assert n2 % FFT_A_N2_BLOCK == 0 and d % FFT_A_COLS == 0 and (s // WIN_BLOCK) % GQA_NQ == 0

    chan, m1, dmat = _dft_tables(s)
    half = MLA_ROPE // 2
    inv_freq = ROPE_THETA ** (-jnp.arange(half, dtype=F32) / half)
    pos_dense = jnp.repeat(positions.reshape(t // ROPE_PACK, ROPE_PACK).astype(F32), half, axis=1)
    invf_dense = jnp.tile(inv_freq, ROPE_PACK)[None, :]
    src = jnp.arange(LANES, dtype=jnp.int32)[None, :, None]
    dst = jnp.arange(LANES, dtype=jnp.int32)[None, None, :]
    slot = jnp.arange(ROPE_PACK, dtype=jnp.int32)[:, None, None]
    place = ((src // half == slot) & (dst >= MLA_NOPE) & (dst < MLA_QK)
             & ((dst - MLA_NOPE) % half == src % half)).astype(BF16)
    pos_blocks = positions.reshape(b * (s // WIN_BLOCK), 1, WIN_BLOCK)
    vbias = jnp.zeros((MLA_HEADS, HEAD_PAD), F32).at[:, MLA_V].set(1.0).reshape(1, -1)
    slopes = tuple(float(v) for v in
                   2.0 ** (-8.0 * (np.arange(GQA_Q_HEADS, dtype=np.float32) + 1.0) / GQA_Q_HEADS))
    r1 = lambda v: v.reshape(1, -1)
    tail_w = [w.astype(BF16) for w in (w_branch_a, w_branch_b, w_branch_c, w_out, w_ffn_gate,
                                       w_ffn_up, w_ffn_down, w_ple_proj, w_ple_gate)]
    wa_all, wb_all, wc_all, wo_all, wg_all, wu_all, wd_all, wp_all, wpg_all = tail_w
    w_in_p, wq, wk, wv = _prep_weights(w_in, w_uq, w_ukv)

    rope_cos, rope_sin = _rope_table(pos_dense, invf_dense, place)

    x2 = x.reshape(t, d)
    for i in range(depth):
        qkv, gates, wr, wi, q, k, v = _inproj(
            x2, r1(norm_mix_pre[i]), w_in_p, chan, rope_cos, rope_sin,
            r1(mla_q_norm[i]), r1(mla_kv_norm[i]), wq, wk, wv, vbias, i)
        hw = MLA_HEADS * HEAD_PAD
        o_a = _flash(q.reshape(b, s, hw), k.reshape(b, s, hw), v.reshape(b, s, hw))

        ar, ai = _fft_a(wr.reshape(b, n1, n2, d), wi.reshape(b, n1, n2, d), m1)
        o_b = _fft_b(ar, ai, dmat)

        o_c = _gqa(qkv.reshape(b, s, -1), pos_blocks, gqa_sink[i], slopes)

        x2 = _token_tail(
            x2, o_a.reshape(t, -1), o_b.reshape(t, d), o_c.reshape(t, -1), gates,
            p, wa_all, wb_all, wc_all, wo_all,
            r1(norm_mix_post[i]), r1(norm_ffn_pre[i]), wg_all, wu_all,
            wd_all, r1(norm_ffn_post[i]), wp_all, wpg_all,
            r1(norm_ple[i]), i)
    return x2.reshape(b, s, d)
```
